```python
import math
import jax, jax.numpy as jnp
from jax import lax
import numpy as np

D_MODEL = 1024
BATCH = 32
SEQ = 2048
DEPTH = 1
DEC_BATCH = 16
DEC_SEQ = 32
PAST_LEN = 2048

CHUNK = 64
A_HEADS = 4
A_HEAD_DIM = 256
A_WIDTH = A_HEADS * A_HEAD_DIM
B_HEADS = 8
B_HEAD_DIM = 128
B_WIDTH = B_HEADS * B_HEAD_DIM
IDX_HEADS = 8
IDX_DIM = 64
TOPK_MAX = 256
Q_BLOCK = 16
N_BUCKETS = 32
MAX_DISTANCE = 128
LN_EPS = 1e-5
HEAD_NORM_EPS = 1e-6
DN_ALPHA = (2 * DEPTH) ** 0.25
DN_BETA = (8 * DEPTH) ** -0.25

COL_SIZES = (A_WIDTH, A_WIDTH, A_WIDTH, A_WIDTH, A_WIDTH, A_HEADS, A_HEADS,
             B_WIDTH, B_WIDTH, B_WIDTH, B_WIDTH, IDX_HEADS * IDX_DIM, IDX_DIM, IDX_HEADS,
             D_MODEL, D_MODEL)
IN_COLS = sum(COL_SIZES)

kernel_name = 'hybrid_mlstm_dsa_stream_step'


def layer_norm(x, g, b):
    xf = x.astype(jnp.float32)
    mu = xf.mean(-1, keepdims=True)
    var = jnp.square(xf - mu).mean(-1, keepdims=True)
    return ((xf - mu) * lax.rsqrt(var + LN_EPS) * g.astype(jnp.float32) + b.astype(jnp.float32)).astype(x.dtype)


def t5_bucket(rel):
    half = N_BUCKETS // 2
    max_exact = half // 2
    ret = jnp.where(rel > 0, half, 0)
    n = jnp.abs(rel)
    nf = jnp.maximum(n, 1).astype(jnp.float32)
    large = max_exact + (jnp.log(nf / max_exact) / math.log(MAX_DISTANCE / max_exact)
                         * (half - max_exact)).astype(jnp.int32)
    large = jnp.minimum(large, half - 1)
    return ret + jnp.where(n < max_exact, n, large)


def project(x, w_in, b_gates):
    B, T, _ = x.shape
    p = x @ w_in
    cuts = [int(c) for c in np.cumsum(COL_SIZES)[:-1]]
    (qa, ka, va, oa, za, ia, fa, qb, kb, vb, zb, qi, ki, wi, ga, gb) = jnp.split(p, cuts, axis=-1)
    ig = ia.astype(jnp.float32) + b_gates[:A_HEADS].astype(jnp.float32)
    lf = jax.nn.log_sigmoid(fa.astype(jnp.float32) + b_gates[A_HEADS:].astype(jnp.float32))
    mlstm_in = (qa.reshape(B, T, A_HEADS, A_HEAD_DIM),
                ka.reshape(B, T, A_HEADS, A_HEAD_DIM) * (A_HEAD_DIM ** -0.5),
                va.reshape(B, T, A_HEADS, A_HEAD_DIM), ig, lf)
    attn_in = (qb.reshape(B, T, B_HEADS, B_HEAD_DIM), kb.reshape(B, T, B_HEADS, B_HEAD_DIM),
               vb.reshape(B, T, B_HEADS, B_HEAD_DIM), qi.reshape(B, T, IDX_HEADS, IDX_DIM), ki, wi)
    gates = (oa, za, zb, ga, gb)
    return mlstm_in, attn_in, gates


def mlstm_chunk(carry, inp):
    C, n, m = carry
    q, k, v, ig, lf = inp
    q = q.astype(jnp.float32)
    k = k.astype(jnp.float32)
    v = v.astype(jnp.float32)
    L = q.shape[1]
    b = jnp.swapaxes(jnp.cumsum(lf, axis=1), 1, 2)
    igt = jnp.swapaxes(ig, 1, 2)
    causal = jnp.tril(jnp.ones((L, L), dtype=bool))
    dmat = jnp.where(causal, b[..., :, None] - b[..., None, :] + igt[..., None, :], -jnp.inf)
    a_inter = b + m[..., None]
    m_j = jnp.maximum(a_inter, dmat.max(-1))
    wmat = jnp.exp(dmat - m_j[..., None])
    inter = jnp.exp(a_inter - m_j)
    s = jnp.einsum('blhd,bshd->bhls', q, k) * wmat
    num = (jnp.einsum('bhls,bshd->blhd', s, v)
           + jnp.einsum('blhk,bhkv->blhv', q, C) * jnp.swapaxes(inter, 1, 2)[..., None])
    den = s.sum(-1) + inter * jnp.einsum('blhk,bhk->bhl', q, n)
    h = num / jnp.swapaxes(jnp.maximum(jnp.abs(den), jnp.exp(-m_j)), 1, 2)[..., None]
    b_last = b[..., -1]
    g = b_last[..., None] - b + igt
    m_new = jnp.maximum(b_last + m, g.max(-1))
    decay = jnp.exp(b_last + m - m_new)
    w_s = jnp.exp(g - m_new[..., None])
    C_new = decay[..., None, None] * C + jnp.einsum('bhs,bshk,bshv->bhkv', w_s, k, v)
    n_new = decay[..., None] * n + jnp.einsum('bhs,bshk->bhk', w_s, k)
    return (C_new, n_new, m_new), h


def mlstm_prompt(q, k, v, ig, lf):
    B, T, H, D = q.shape
    nc = T // CHUNK
    to_chunks = lambda a: jnp.swapaxes(a.reshape((B, nc, CHUNK) + a.shape[2:]), 0, 1)
    init = (jnp.zeros((B, H, D, D), jnp.float32), jnp.zeros((B, H, D), jnp.float32),
            jnp.zeros((B, H), jnp.float32))
    state, h = lax.scan(mlstm_chunk, init, tuple(to_chunks(a) for a in (q, k, v, ig, lf)))
    return jnp.swapaxes(h, 0, 1).reshape(B, T, H, D), state


def dsa_block(q, qi, wi, qpos, k, v, ki, rel_bias, k_sel):
    S = k.shape[1]
    sc = jnp.einsum('bqhd,bsd->bqhs', qi, ki).astype(jnp.float32) * (IDX_DIM ** -0.5)
    score = jnp.einsum('bqh,bqhs->bqs', wi.astype(jnp.float32) * (IDX_HEADS ** -0.5), jax.nn.relu(sc))
    limit = (qpos // CHUNK + 1) * CHUNK
    adm = jnp.arange(S, dtype=jnp.int32)[None, :] < limit[:, None]
    score = jnp.where(adm[None], score, -jnp.inf)
    _, idx = lax.top_k(score, k_sel)
    gather = jax.vmap(lambda rows, ii: rows[ii])
    kg = gather(k, idx)
    vg = gather(v, idx)
    logits = jnp.einsum('bqhd,bqkhd->bhqk', q, kg).astype(jnp.float32) * (B_HEAD_DIM ** -0.5)
    bias = rel_bias[t5_bucket(idx - qpos[None, :, None])]
    logits = logits + jnp.moveaxis(bias, -1, 1).astype(jnp.float32)
    ok = idx < limit[None, :, None]
    logits = jnp.where(ok[:, None], logits, -jnp.inf)
    p = jax.nn.softmax(logits, axis=-1)
    return jnp.einsum('bhqk,bqkhd->bqhd', p.astype(v.dtype), vg)


def dsa_prompt(q, k, v, qi, ki, wi, rel_bias):
    B, T = q.shape[:2]
    k_sel = min(TOPK_MAX, T // 4)
    nb = T // Q_BLOCK
    to_blocks = lambda a: jnp.swapaxes(a.reshape((B, nb, Q_BLOCK) + a.shape[2:]), 0, 1)
    pos_blocks = jnp.arange(T, dtype=jnp.int32).reshape(nb, Q_BLOCK)
    out = lax.map(lambda a: dsa_block(a[0], a[1], a[2], a[3], k, v, ki, rel_bias, k_sel),
                  (to_blocks(q), to_blocks(qi), to_blocks(wi), pos_blocks))
    return jnp.swapaxes(out, 0, 1).reshape(B, T, B_HEADS, B_HEAD_DIM)


def dsa_sample(q, k_new, v_new, qi, ki_new, wi, cache_k, cache_v, cache_idx_k, rel_bias):
    past = cache_k.shape[1]
    T = q.shape[1]
    k = jnp.concatenate([cache_k, k_new], axis=1)
    v = jnp.concatenate([cache_v, v_new], axis=1)
    ki = jnp.concatenate([cache_idx_k, ki_new], axis=1)
    k_sel = min(TOPK_MAX, (past + T) // 4)
    qpos = past + jnp.arange(T, dtype=jnp.int32)
    return dsa_block(q, qi, wi, qpos, k, v, ki, rel_bias, k_sel)


def merge(x, h_a, o_b, gates, a_norm_g, w_out_a, w_out_b, w_o, ln_g, ln_b):
    oa, za, zb, ga, gb = gates
    B, T, _ = x.shape
    mu = h_a.mean(-1, keepdims=True)
    var = jnp.square(h_a - mu).mean(-1, keepdims=True)
    hn = ((h_a - mu) * lax.rsqrt(var + HEAD_NORM_EPS)).reshape(B, T, A_WIDTH) * a_norm_g.astype(jnp.float32)
    branch_a = (hn * jax.nn.sigmoid(oa.astype(jnp.float32))).astype(x.dtype) * jax.nn.silu(za)
    branch_b = o_b.reshape(B, T, B_WIDTH) * jax.nn.silu(zb)
    mixed = jax.nn.sigmoid(ga) * (branch_a @ w_out_a) + jax.nn.sigmoid(gb) * (branch_b @ w_out_b)
    return layer_norm(DN_ALPHA * x + mixed @ w_o, ln_g, ln_b)


def setup_inputs(seed: int = 0) -> dict:
    key = jax.random.key(seed)
    ks = jax.random.split(key, 20)
    nrm = lambda i, shape: jax.random.normal(ks[i], shape, jnp.float32)
    return {
        'x_prompt': nrm(0, (BATCH, SEQ, D_MODEL)),
        'x_sample': nrm(1, (DEC_BATCH, DEC_SEQ, D_MODEL)),
        'cache_k': nrm(2, (DEC_BATCH, PAST_LEN, B_HEADS, B_HEAD_DIM)),
        'cache_v': nrm(3, (DEC_BATCH, PAST_LEN, B_HEADS, B_HEAD_DIM)),
        'cache_idx_k': nrm(4, (DEC_BATCH, PAST_LEN, IDX_DIM)),
        'state_C': nrm(5, (DEC_BATCH, A_HEADS, A_HEAD_DIM, A_HEAD_DIM)) * 0.1,
        'state_n': nrm(6, (DEC_BATCH, A_HEADS, A_HEAD_DIM)) * 0.5,
        'state_m': nrm(7, (DEC_BATCH, A_HEADS)),
        'w_in': nrm(8, (D_MODEL, IN_COLS)) * (D_MODEL ** -0.5),
        'b_gates': jnp.concatenate([0.1 * nrm(9, (A_HEADS,)),
                                    jnp.linspace(3.0, 6.0, A_HEADS, dtype=jnp.float32) + 0.01 * nrm(10, (A_HEADS,))]),
        'a_norm_g': 1.0 + 0.01 * nrm(11, (A_WIDTH,)),
        'w_out_a': nrm(12, (A_WIDTH, D_MODEL)) * (A_WIDTH ** -0.5) * DN_BETA,
        'w_out_b': nrm(13, (B_WIDTH, D_MODEL)) * (B_WIDTH ** -0.5) * DN_BETA,
        'w_o': nrm(14, (D_MODEL, D_MODEL)) * (D_MODEL ** -0.5) * DN_BETA,
        'rel_bias': 0.2 * nrm(15, (N_BUCKETS, B_HEADS)),
        'ln_g': 1.0 + 0.01 * nrm(16, (D_MODEL,)),
        'ln_b': 0.01 * nrm(17, (D_MODEL,)),
    }


def reference(x_prompt, x_sample, cache_k, cache_v, cache_idx_k, state_C, state_n, state_m,
              w_in, b_gates, a_norm_g, w_out_a, w_out_b, w_o, rel_bias, ln_g, ln_b):
    for _ in range(DEPTH):
        (qa, ka, va, ig, lf), (qb, kb, vb, qi, ki, wi), gates_p = project(x_prompt, w_in, b_gates)
        h_a, (c_p, n_p, m_p) = mlstm_prompt(qa, ka, va, ig, lf)
        o_b = dsa_prompt(qb, kb, vb, qi, ki, wi, rel_bias)
        y_prompt = merge(x_prompt, h_a, o_b, gates_p, a_norm_g, w_out_a, w_out_b, w_o, ln_g, ln_b)
        (qa_s, ka_s, va_s, ig_s, lf_s), (qb_s, kb_s, vb_s, qi_s, ki_s, wi_s), gates_s = project(x_sample, w_in, b_gates)
        carry = (state_C.astype(jnp.float32), state_n.astype(jnp.float32), state_m.astype(jnp.float32))
        (c_s, n_s, m_s), h_a_s = mlstm_chunk(carry, (qa_s, ka_s, va_s, ig_s, lf_s))
        o_b_s = dsa_sample(qb_s, kb_s, vb_s, qi_s, ki_s, wi_s, cache_k, cache_v, cache_idx_k, rel_bias)
        y_sample = merge(x_sample, h_a_s, o_b_s, gates_s, a_norm_g, w_out_a, w_out_b, w_o, ln_g, ln_b)
    return (y_prompt, y_sample, kb, vb, ki, c_p, n_p, m_p, kb_s, vb_s, ki_s, c_s, n_s, m_s)
```

```python
import functools
import math

import jax
import jax.numpy as jnp
from jax import lax
from jax.experimental import pallas as pl
from jax.experimental.pallas import tpu as pltpu

F32 = jnp.float32
BF16 = jnp.bfloat16
I32 = jnp.int32

D_MODEL = 1024
DEPTH = 1
CHUNK = 64
A_HEADS = 4
A_HEAD_DIM = 256
A_WIDTH = A_HEADS * A_HEAD_DIM
B_HEADS = 8
B_HEAD_DIM = 128
B_WIDTH = B_HEADS * B_HEAD_DIM
IDX_HEADS = 8
IDX_DIM = 64
TOPK_MAX = 256
N_BUCKETS = 32
MAX_DISTANCE = 128
LN_EPS = 1e-5
HEAD_NORM_EPS = 1e-6
DN_ALPHA = (2 * DEPTH) ** 0.25

COL_SIZES = (A_WIDTH, A_WIDTH, A_WIDTH, A_WIDTH, A_WIDTH, A_HEADS, A_HEADS,
             B_WIDTH, B_WIDTH, B_WIDTH, B_WIDTH, IDX_HEADS * IDX_DIM, IDX_DIM, IDX_HEADS,
             D_MODEL, D_MODEL)

LANES = 128
MXU_DIM = 256
VMEM_LIMIT_BYTES = 56 * 1024 * 1024

TILE = MXU_DIM
PROJ_TM = 1024
PROJ_TN = 1024
MERGE_TM = 256

SMALL_IG = 0
SMALL_FG = 8
SMALL_WI = 16

INT_MIN = -2 ** 31
NEG_INF = float("-inf")
NT_DIMS = (((1,), (1,)), ((), ()))
TN_DIMS = (((0,), (0,)), ((), ()))


def _cparams(*sem):
    return pltpu.CompilerParams(dimension_semantics=sem, vmem_limit_bytes=VMEM_LIMIT_BYTES)


def _proj_kernel(x_ref, w_ref, *o_refs):
    acc = jnp.dot(x_ref[...].astype(BF16), w_ref[...], preferred_element_type=F32)
    for o_ref in o_refs:
        o_ref[...] = acc.astype(o_ref.dtype)


def _project(x2d, w, out_dtypes, name):
    m, k = x2d.shape
    n = w.shape[1]
    tm = min(PROJ_TM, m)
    tn = min(PROJ_TN, n)
    assert m % tm == 0 and n % tn == 0
    outs = pl.pallas_call(
        _proj_kernel,
        grid=(m // tm, n // tn),
        in_specs=[pl.BlockSpec((tm, k), lambda i, j: (i, 0)),
                  pl.BlockSpec((k, tn), lambda i, j: (0, j))],
        out_specs=[pl.BlockSpec((tm, tn), lambda i, j: (i, j)) for _ in out_dtypes],
        out_shape=[jax.ShapeDtypeStruct((m, n), dt) for dt in out_dtypes],
        compiler_params=_cparams("parallel", "arbitrary"),
        name=name,
    )(x2d, w)
    return outs


def _mlstm_kernel(q_ref, k_ref, v_ref, sm_ref, bg_ref, c0_ref, n0_ref, m0_ref,
                  hn_ref, c_ref, n_ref, m_ref, *, t_valid):
    L = q_ref.shape[0]
    c = pl.program_id(1)

    @pl.when(c == 0)
    def _():
        c_ref[...] = c0_ref[...]
        n_ref[...] = n0_ref[...]
        m_ref[...] = m0_ref[...]

    z = sm_ref[...] + bg_ref[...]
    col = lax.broadcasted_iota(I32, z.shape, 1)
    tok = c * L + lax.broadcasted_iota(I32, z.shape, 0)
    is_ig = col < SMALL_FG
    log_sig = jnp.minimum(z, 0.0) - jnp.log1p(jnp.exp(-jnp.abs(z)))
    gl = jnp.where(is_ig, z, log_sig)
    gl = jnp.where(tok < t_valid, gl, jnp.where(is_ig, NEG_INF, 0.0))
    gl_t = gl.T
    lane = lax.broadcasted_iota(I32, gl_t.shape, 1)
    cum_t = gl_t
    shift = 1
    while shift < L:
        cum_t = cum_t + jnp.where(lane >= shift, pltpu.roll(cum_t, shift, 1), 0.0)
        shift *= 2
    cum = cum_t.T

    row_i = lax.broadcasted_iota(I32, (L, L), 0)
    col_i = lax.broadcasted_iota(I32, (L, L), 1)
    causal = col_i <= row_i
    k_scale = A_HEAD_DIM ** -0.5

    for h in range(A_HEADS):
        sl = slice(h * A_HEAD_DIM, (h + 1) * A_HEAD_DIM)
        b_col = cum[:, SMALL_FG + h:SMALL_FG + h + 1]
        b_row = cum_t[SMALL_FG + h:SMALL_FG + h + 1, :]
        ig_col = gl[:, SMALL_IG + h:SMALL_IG + h + 1]
        ig_row = gl_t[SMALL_IG + h:SMALL_IG + h + 1, :]
        m_old = m_ref[h][0:1, 0:1]
        q = q_ref[:, sl]
        k = k_ref[:, sl]
        v = v_ref[:, sl]
        c_old = c_ref[h]
        n_old = n_ref[h]

        dmat = jnp.where(causal, b_col - b_row + ig_row, NEG_INF)
        a_inter = b_col + m_old
        m_j = jnp.maximum(a_inter, jnp.max(dmat, axis=1, keepdims=True))
        wmat = jnp.exp(dmat - m_j)
        inter = jnp.exp(a_inter - m_j)
        s = lax.dot_general(q, k, NT_DIMS, preferred_element_type=F32) * k_scale * wmat
        num = (jnp.dot(s.astype(BF16), v, preferred_element_type=F32)
               + jnp.dot(q, c_old.astype(BF16), preferred_element_type=F32) * inter)
        qn = jnp.sum(q.astype(F32) * n_old[0:1, :], axis=1, keepdims=True)
        den = jnp.sum(s, axis=1, keepdims=True) + inter * qn
        hh = num / jnp.maximum(jnp.abs(den), jnp.exp(-m_j))
        mu = jnp.mean(hh, axis=1, keepdims=True)
        var = jnp.mean(jnp.square(hh - mu), axis=1, keepdims=True)
        hn_ref[:, sl] = (hh - mu) * lax.rsqrt(var + HEAD_NORM_EPS)

        b_last = b_col[L - 1:L, :]
        g_col = b_last - b_col + ig_col
        g_row = b_last - b_row + ig_row
        m_new = jnp.maximum(b_last + m_old, jnp.max(g_row, axis=1, keepdims=True))
        decay = jnp.exp(b_last + m_old - m_new)
        w_col = jnp.exp(g_col - m_new) * k_scale
        kw = k.astype(F32) * w_col
        c_ref[h] = decay * c_old + jnp.dot(kw.T.astype(BF16), v, preferred_element_type=F32)
        n_ref[h] = decay * n_old + jnp.sum(kw, axis=0, keepdims=True)
        m_ref[h] = jnp.broadcast_to(m_new, (8, LANES))


def _mlstm(qkv, small, bg, c0, n0, m0, t_valid):
    b, t, _ = qkv.shape
    assert t % TILE == 0
    n0x = jnp.broadcast_to(n0[:, :, None, :], (b, A_HEADS, 8, A_HEAD_DIM))
    m0x = jnp.broadcast_to(m0[:, :, None, None], (b, A_HEADS, 8, LANES))
    qkv_spec = lambda j: pl.BlockSpec((None, TILE, A_WIDTH), lambda bi, ci: (bi, ci, j))
    st_c = pl.BlockSpec((None, A_HEADS, A_HEAD_DIM, A_HEAD_DIM), lambda bi, ci: (bi, 0, 0, 0))
    st_n = pl.BlockSpec((None, A_HEADS, 8, A_HEAD_DIM), lambda bi, ci: (bi, 0, 0, 0))
    st_m = pl.BlockSpec((None, A_HEADS, 8, LANES), lambda bi, ci: (bi, 0, 0, 0))
    hn, c_new, n_new, m_new = pl.pallas_call(
        functools.partial(_mlstm_kernel, t_valid=t_valid),
        grid=(b, t // TILE),
        in_specs=[qkv_spec(0), qkv_spec(1), qkv_spec(2),
                  pl.BlockSpec((None, TILE, LANES), lambda bi, ci: (bi, ci, 0)),
                  pl.BlockSpec((1, LANES), lambda bi, ci: (0, 0)),
                  st_c, st_n, st_m],
        out_specs=[pl.BlockSpec((None, TILE, A_WIDTH), lambda bi, ci: (bi, ci, 0)),
                   st_c, st_n, st_m],
        out_shape=[jax.ShapeDtypeStruct((b, t, A_WIDTH), F32),
                   jax.ShapeDtypeStruct((b, A_HEADS, A_HEAD_DIM, A_HEAD_DIM), F32),
                   jax.ShapeDtypeStruct((b, A_HEADS, 8, A_HEAD_DIM), F32),
                   jax.ShapeDtypeStruct((b, A_HEADS, 8, LANES), F32)],
        compiler_params=_cparams("parallel", "arbitrary"),
        name="mlstm",
    )(qkv, qkv, qkv, small, bg, c0, n0x, m0x)
    return hn, c_new, n_new[:, :, 0, :], m_new[:, :, 0, 0]


def _t5_bucket(rel):
    half = N_BUCKETS // 2
    max_exact = half // 2
    ret = jnp.where(rel > 0, half, 0)
    n = jnp.abs(rel)
    nf = jnp.maximum(n, 1).astype(F32)
    large = max_exact + (jnp.log(nf / max_exact) / math.log(MAX_DISTANCE / max_exact)
                         * (half - max_exact)).astype(I32)
    large = jnp.minimum(large, half - 1)
    return ret + jnp.where(n < max_exact, n, large)


def _bias_kernel(rb_ref, out_ref):
    kk = lax.broadcasted_iota(I32, (TILE, TILE), 0)
    qq = lax.broadcasted_iota(I32, (TILE, TILE), 1)
    for ti, off in enumerate((0, -TILE, -2 * TILE)):
        bucket = _t5_bucket(kk + off - qq)
        for h in range(B_HEADS):
            acc = jnp.zeros((TILE, TILE), F32)
            for bkt in range(N_BUCKETS):
                acc = jnp.where(bucket == bkt, rb_ref[bkt, h], acc)
            out_ref[h, ti] = acc


def _bias_tiles(rel_bias):
    return pl.pallas_call(
        _bias_kernel,
        in_specs=[pl.BlockSpec(memory_space=pltpu.SMEM)],
        out_specs=pl.BlockSpec(memory_space=pltpu.VMEM),
        out_shape=jax.ShapeDtypeStruct((B_HEADS, 3, TILE, TILE), F32),
        compiler_params=pltpu.CompilerParams(vmem_limit_bytes=VMEM_LIMIT_BYTES),
        name="bias_tiles",
    )(rel_bias.astype(F32))


def _dsa_kernel(q_ref, qi_ref, sm_ref, k_ref, v_ref, ki_ref, bias_ref, o_ref,
                keys_ref, madd_ref, lg_ref, thr_ref, jsel_ref, *, pos0, s_valid, k_sel):
    tq = q_ref.shape[0]
    tk = TILE
    t0 = pos0 + pl.program_id(1) * tq
    nkb = t0 // tk + 1
    qpos = t0 + lax.broadcasted_iota(I32, (1, tq), 1)
    limit = jnp.minimum((qpos // CHUNK + 1) * CHUNK, s_valid)
    blk_iota = lax.broadcasted_iota(I32, (tk, tq), 0)
    idx_bits = int(k_ref.shape[0]).bit_length()

    def rows(kb):
        return pl.ds(pl.multiple_of(kb * tk, tk), tk)

    w_t = sm_ref[...].T
    w_heads = [w_t[SMALL_WI + h:SMALL_WI + h + 1, :] * (IDX_HEADS ** -0.5) * (IDX_DIM ** -0.5)
               for h in range(IDX_HEADS)]

    def score_body(kb, carry):
        ki = ki_ref[rows(kb), :]
        sc = jnp.zeros((tk, tq), F32)
        for h in range(IDX_HEADS):
            qh = qi_ref[:, h * IDX_DIM:(h + 1) * IDX_DIM]
            d = lax.dot_general(ki, qh, NT_DIMS, preferred_element_type=F32)
            sc = sc + w_heads[h] * jnp.maximum(d, 0.0)
        sc = jnp.where(sc == 0.0, 0.0, sc)
        bits = lax.bitcast_convert_type(sc, I32)
        key = bits ^ ((bits >> 31) & 0x7FFFFFFF)
        adm = (kb * tk + blk_iota) < limit
        keys_ref[rows(kb), :] = jnp.where(adm, key, INT_MIN)
        return carry

    lax.fori_loop(0, nkb, score_body, 0)

    thr_ref[...] = jnp.full(thr_ref.shape, INT_MIN, I32)
    jsel_ref[...] = jnp.full(jsel_ref.shape, 2 ** 30, I32)

    def count(pred):
        def body(kb, acc):
            hit = pred(keys_ref[rows(kb), :], kb * tk + blk_iota)
            return acc + jnp.sum(jnp.where(hit, 1.0, 0.0), axis=0, keepdims=True)
        return lax.fori_loop(0, nkb, body, jnp.zeros((1, tq), F32))

    @pl.when(nkb * tk > k_sel)
    def _():
        kf = float(k_sel)
        zero = jnp.zeros((1, tq), I32)
        cnt = count(lambda blk, _: blk >= zero)
        thr = jnp.where(cnt >= kf, zero, jnp.full((1, tq), INT_MIN, I32))

        def bit_body(i, thr):
            cand = thr + jnp.left_shift(jnp.int32(1), 30 - i)
            cnt = count(lambda blk, _: blk >= cand)
            return jnp.where(cnt >= kf, cand, thr)

        thr = lax.fori_loop(0, 31, bit_body, thr)
        thr_ref[...] = jnp.broadcast_to(thr, thr_ref.shape)
        n_gt = count(lambda blk, _: blk > thr)
        n_ge = count(lambda blk, _: blk >= thr)
        need = kf - n_gt
        split = jnp.max(jnp.where(n_ge - n_gt > need, 1.0, 0.0)) > 0.0

        @pl.when(split)
        def _():
            def jbit_body(i, j0):
                cand = j0 + jnp.left_shift(jnp.int32(1), idx_bits - 1 - i)
                f = count(lambda blk, idx: (blk == thr) & (idx < cand))
                return jnp.where(f < need, cand, j0)

            j0 = lax.fori_loop(0, idx_bits, jbit_body, jnp.zeros((1, tq), I32))
            jsel_ref[...] = jnp.broadcast_to(j0 + 1, jsel_ref.shape)

    thr = thr_ref[0:1, :]
    jsel = jsel_ref[0:1, :]

    def mask_body(kb, carry):
        blk = keys_ref[rows(kb), :]
        idx = kb * tk + blk_iota
        sel = ((blk > thr) | ((blk == thr) & (idx < jsel))) & (idx < limit)
        madd_ref[rows(kb), :] = jnp.where(sel, 0.0, NEG_INF)
        return carry

    lax.fori_loop(0, nkb, mask_body, 0)

    scale = B_HEAD_DIM ** -0.5
    ones_ext = jnp.ones((tk, LANES), BF16)
    for h in range(B_HEADS):
        sl = slice(h * B_HEAD_DIM, (h + 1) * B_HEAD_DIM)
        qh = q_ref[:, sl]

        def logit_body(kb, m_run):
            kh = k_ref[rows(kb), sl]
            lt = lax.dot_general(kh, qh, NT_DIMS, preferred_element_type=F32) * scale
            tile_id = jnp.minimum(nkb - 1 - kb, 2)
            lt = lt + bias_ref[h, tile_id] + madd_ref[rows(kb), :]
            lg_ref[rows(kb), :] = lt
            return jnp.maximum(m_run, jnp.max(lt, axis=0, keepdims=True))

        m_run = lax.fori_loop(0, nkb, logit_body, jnp.full((1, tq), NEG_INF, F32))

        def pv_body(kb, acc):
            p = jnp.exp(lg_ref[rows(kb), :] - m_run).astype(BF16)
            v_ext = jnp.concatenate([v_ref[rows(kb), sl], ones_ext], axis=1)
            return acc + lax.dot_general(p, v_ext, TN_DIMS, preferred_element_type=F32)

        acc = lax.fori_loop(0, nkb, pv_body, jnp.zeros((tq, 2 * B_HEAD_DIM), F32))
        o_ref[:, sl] = acc[:, :B_HEAD_DIM] / acc[:, B_HEAD_DIM:B_HEAD_DIM + 1]


def _dsa(q, qi, small, k, v, ki, bias, pos0, s_valid, k_sel):
    b, tq_all, _ = q.shape
    s_pad = k.shape[1]
    assert tq_all % TILE == 0 and s_pad % TILE == 0 and pos0 % TILE == 0
    assert pos0 + tq_all <= s_pad
    qspec = lambda w: pl.BlockSpec((None, TILE, w), lambda bi, ji: (bi, ji, 0))
    kspec = lambda w: pl.BlockSpec((None, s_pad, w), lambda bi, ji: (bi, 0, 0))
    return pl.pallas_call(
        functools.partial(_dsa_kernel, pos0=pos0, s_valid=s_valid, k_sel=k_sel),
        grid=(b, tq_all // TILE),
        in_specs=[qspec(B_WIDTH), qspec(IDX_HEADS * IDX_DIM), qspec(LANES),
                  kspec(B_WIDTH), kspec(B_WIDTH), kspec(IDX_DIM),
                  pl.BlockSpec((B_HEADS, 3, TILE, TILE), lambda bi, ji: (0, 0, 0, 0))],
        out_specs=qspec(B_WIDTH),
        out_shape=jax.ShapeDtypeStruct((b, tq_all, B_WIDTH), F32),
        scratch_shapes=[pltpu.VMEM((s_pad, TILE), I32),
                        pltpu.VMEM((s_pad, TILE), F32),
                        pltpu.VMEM((s_pad, TILE), F32),
                        pltpu.VMEM((8, TILE), I32),
                        pltpu.VMEM((8, TILE), I32)],
        compiler_params=_cparams("parallel", "arbitrary"),
        name="dsa",
    )(q, qi, small, k, v, ki, bias)


def _merge_kernel(x_ref, hn_ref, ob_ref, wg_ref, woa_ref, wob_ref, wo_ref, vec_ref, y_ref):
    x = x_ref[...]
    xb = x.astype(BF16)
    gate = lambda i: jnp.dot(xb, wg_ref[i], preferred_element_type=F32)
    hn = hn_ref[...] * vec_ref[0:1, :]
    branch_a = (hn * jax.nn.sigmoid(gate(0))) * jax.nn.silu(gate(1))
    branch_b = ob_ref[...] * jax.nn.silu(gate(2))
    mixed = (jax.nn.sigmoid(gate(3)) * jnp.dot(branch_a.astype(BF16), woa_ref[...], preferred_element_type=F32)
             + jax.nn.sigmoid(gate(4)) * jnp.dot(branch_b.astype(BF16), wob_ref[...], preferred_element_type=F32))
    y = DN_ALPHA * x + jnp.dot(mixed.astype(BF16), wo_ref[...], preferred_element_type=F32)
    mu = jnp.mean(y, axis=1, keepdims=True)
    var = jnp.mean(jnp.square(y - mu), axis=1, keepdims=True)
    y_ref[...] = (y - mu) * lax.rsqrt(var + LN_EPS) * vec_ref[1:2, :] + vec_ref[2:3, :]


def _merge(x2d, hn2d, ob2d, wg, woa, wob, wo, vec):
    m, d = x2d.shape
    tm = min(MERGE_TM, m)
    assert m % tm == 0
    row = pl.BlockSpec((tm, d), lambda i: (i, 0))
    const = lambda shape: pl.BlockSpec(shape, lambda i: (0,) * len(shape))
    return pl.pallas_call(
        _merge_kernel,
        grid=(m // tm,),
        in_specs=[row, row, row, const(wg.shape), const(woa.shape), const(wob.shape),
                  const(wo.shape), const(vec.shape)],
        out_specs=row,
        out_shape=jax.ShapeDtypeStruct((m, d), F32),
        compiler_params=_cparams("parallel"),
        name="merge",
    )(x2d, hn2d, ob2d, wg, woa, wob, wo, vec)


def _split_weights(w_in, b_gates):
    cuts = [0]
    for c in COL_SIZES:
        cuts.append(cuts[-1] + c)
    cols = [w_in[:, cuts[i]:cuts[i + 1]] for i in range(len(COL_SIZES))]
    (qa, ka, va, oa, za, ia, fa, qb, kb, vb, zb, qi, ki, wi, ga, gb) = cols
    pad = lambda w, n: jnp.pad(w, ((0, 0), (0, n - w.shape[1])))
    small = jnp.concatenate([pad(ia, SMALL_FG - SMALL_IG), pad(fa, SMALL_WI - SMALL_FG),
                             pad(wi, LANES - SMALL_WI)], axis=1)
    bg = jnp.concatenate([jnp.pad(b_gates[:A_HEADS], (0, SMALL_FG - SMALL_IG - A_HEADS)),
                          jnp.pad(b_gates[A_HEADS:], (0, LANES - SMALL_FG - A_HEADS))])[None, :]
    c16 = lambda w: w.astype(BF16)
    return dict(qkv_a=c16(jnp.concatenate([qa, ka, va], axis=1)), qb=c16(qb), kb=c16(kb), vb=c16(vb),
                qi=c16(qi), ki=c16(ki), small=c16(small), bg=bg.astype(F32),
                gates=jnp.stack([c16(oa), c16(za), c16(zb), c16(ga), c16(gb)]))


def _pad_rows(a, n):
    return jnp.pad(a, ((0, 0), (0, n - a.shape[1]), (0, 0)))


def _group(x, w, consts, state, cache):
    b, t, d = x.shape
    x2d = x.reshape(b * t, d)
    (qkv_a,) = _project(x2d, w["qkv_a"], (BF16,), "proj_qkv_a")
    (qb,) = _project(x2d, w["qb"], (BF16,), "proj_qb")
    kb32, kb16 = _project(x2d, w["kb"], (F32, BF16), "proj_kb")
    vb32, vb16 = _project(x2d, w["vb"], (F32, BF16), "proj_vb")
    (qi,) = _project(x2d, w["qi"], (BF16,), "proj_qi")
    ki32, ki16 = _project(x2d, w["ki"], (F32, BF16), "proj_ki")
    (small,) = _project(x2d, w["small"], (F32,), "proj_small")
    r3 = lambda a: a.reshape(b, t, a.shape[-1])
    qkv_a, qb, kb16, vb16, qi, ki16, small = map(r3, (qkv_a, qb, kb16, vb16, qi, ki16, small))

    t_pad = -(-t // TILE) * TILE
    c0, n0, m0 = state
    hn, c_new, n_new, m_new = _mlstm(_pad_rows(qkv_a, t_pad), _pad_rows(small, t_pad), w["bg"],
                                     c0, n0, m0, t_valid=t)

    if cache is None:
        pos0, keys_k, keys_v, keys_i = 0, kb16, vb16, ki16
    else:
        cache_k, cache_v, cache_i = cache
        pos0 = cache_k.shape[1]
        flat = lambda a: a.reshape(b, pos0, -1).astype(BF16)
        keys_k = jnp.concatenate([flat(cache_k), kb16], axis=1)
        keys_v = jnp.concatenate([flat(cache_v), vb16], axis=1)
        keys_i = jnp.concatenate([flat(cache_i), ki16], axis=1)
    s_valid = pos0 + t
    s_pad = pos0 + t_pad
    k_sel = min(TOPK_MAX, s_valid // 4)
    ob = _dsa(_pad_rows(qb, t_pad), _pad_rows(qi, t_pad), _pad_rows(small, t_pad),
              _pad_rows(keys_k, s_pad), _pad_rows(keys_v, s_pad), _pad_rows(keys_i, s_pad),
              consts["bias"], pos0=pos0, s_valid=s_valid, k_sel=k_sel)

    y = _merge(x2d, hn[:, :t].reshape(b * t, A_WIDTH), ob[:, :t].reshape(b * t, B_WIDTH),
               w["gates"], consts["woa"], consts["wob"], consts["wo"], consts["vec"])
    return (y.reshape(b, t, d), kb32.reshape(b, t, B_HEADS, B_HEAD_DIM),
            vb32.reshape(b, t, B_HEADS, B_HEAD_DIM), ki32.reshape(b, t, IDX_DIM),
            c_new, n_new, m_new)


def kernel(x_prompt, x_sample, cache_k, cache_v, cache_idx_k, state_C, state_n, state_m,
           w_in, b_gates, a_norm_g, w_out_a, w_out_b, w_o, rel_bias, ln_g, ln_b):
    w = _split_weights(w_in, b_gates)
    consts = dict(bias=_bias_tiles(rel_bias), woa=w_out_a.astype(BF16), wob=w_out_b.astype(BF16),
                  wo=w_o.astype(BF16),
                  vec=jnp.pad(jnp.stack([a_norm_g, ln_g, ln_b]).astype(F32), ((0, 5), (0, 0))))
    bp = x_prompt.shape[0]
    zero_state = (jnp.zeros((bp, A_HEADS, A_HEAD_DIM, A_HEAD_DIM), F32),
                  jnp.zeros((bp, A_HEADS, A_HEAD_DIM), F32), jnp.zeros((bp, A_HEADS), F32))
    y_p, k_p, v_p, i_p, c_p, n_p, m_p = _group(x_prompt, w, consts, zero_state, None)
    y_s, k_s, v_s, i_s, c_s, n_s, m_s = _group(
        x_sample, w, consts,
        (state_C.astype(F32), state_n.astype(F32), state_m.astype(F32)),
        (cache_k, cache_v, cache_idx_k))
    return (y_p, y_s, k_p, v_p, i_p, c_p, n_p, m_p, k_s, v_s, i_s, c_s, n_s, m_s)
```

```python
import functools
import math

import jax
import jax.numpy as jnp
from jax import lax
from jax.experimental import pallas as pl
from jax.experimental.pallas import tpu as pltpu

F32 = jnp.float32
BF16 = jnp.bfloat16
I32 = jnp.int32

D_MODEL = 1024
DEPTH = 1
CHUNK = 64
A_HEADS = 4
A_HEAD_DIM = 256
A_WIDTH = A_HEADS * A_HEAD_DIM
B_HEADS = 8
B_HEAD_DIM = 128
B_WIDTH = B_HEADS * B_HEAD_DIM
IDX_HEADS = 8
IDX_DIM = 64
TOPK_MAX = 256
N_BUCKETS = 32
MAX_DISTANCE = 128
LN_EPS = 1e-5
HEAD_NORM_EPS = 1e-6
DN_ALPHA = (2 * DEPTH) ** 0.25

COL_SIZES = (A_WIDTH, A_WIDTH, A_WIDTH, A_WIDTH, A_WIDTH, A_HEADS, A_HEADS,
             B_WIDTH, B_WIDTH, B_WIDTH, B_WIDTH, IDX_HEADS * IDX_DIM, IDX_DIM, IDX_HEADS,
             D_MODEL, D_MODEL)

LANES = 128
MXU_DIM = 256
VMEM_LIMIT_BYTES = 56 * 1024 * 1024

TILE = MXU_DIM
PROJ_TM = 1024
PROJ_TN = 1024
MERGE_TM = 256

SMALL_IG = 0
SMALL_FG = 8
SMALL_WI = 16

LOG2E = math.log2(math.e)
INT_MIN = -2 ** 31
NEG_INF = float("-inf")
NT_DIMS = (((1,), (1,)), ((), ()))
TN_DIMS = (((0,), (0,)), ((), ()))


def _cparams(*sem):
    return pltpu.CompilerParams(dimension_semantics=sem, vmem_limit_bytes=VMEM_LIMIT_BYTES)


def _proj_kernel(x_ref, w_ref, *o_refs, col_slices):
    acc = jnp.dot(x_ref[...].astype(BF16), w_ref[...], preferred_element_type=F32)
    for o_ref, (c0, cw) in zip(o_refs, col_slices):
        o_ref[...] = acc[:, c0:c0 + cw].astype(o_ref.dtype)


def _project(x2d, w, outs, name):
    m, k = x2d.shape
    n = w.shape[1]
    tm = min(PROJ_TM, m)
    full = all((c0, cw) == (0, n) for c0, cw, _ in outs)
    tn = min(PROJ_TN, n) if full else n
    assert m % tm == 0 and n % tn == 0
    if full:
        out_specs = [pl.BlockSpec((tm, tn), lambda i, j: (i, j)) for _ in outs]
        col_slices = tuple((0, tn) for _ in outs)
    else:
        out_specs = [pl.BlockSpec((tm, cw), lambda i, j: (i, 0)) for _, cw, _ in outs]
        col_slices = tuple((c0, cw) for c0, cw, _ in outs)
    return pl.pallas_call(
        functools.partial(_proj_kernel, col_slices=col_slices),
        grid=(m // tm, n // tn),
        in_specs=[pl.BlockSpec((tm, k), lambda i, j: (i, 0)),
                  pl.BlockSpec((k, tn), lambda i, j: (0, j))],
        out_specs=out_specs,
        out_shape=[jax.ShapeDtypeStruct((m, cw), dt) for _, cw, dt in outs],
        compiler_params=_cparams("parallel", "arbitrary"),
        name=name,
    )(x2d, w)


def _mlstm_kernel(q_ref, k_ref, v_ref, sm_ref, bg_ref, c0_ref, n0_ref, m0_ref,
                  hn_ref, c_ref, n_ref, m_ref, *, t_valid):
    L = q_ref.shape[0]
    c = pl.program_id(1)

    @pl.when(c == 0)
    def _():
        c_ref[...] = c0_ref[...]
        n_ref[...] = n0_ref[...]
        m_ref[...] = m0_ref[...]

    z = sm_ref[...] + bg_ref[...]
    col = lax.broadcasted_iota(I32, z.shape, 1)
    tok = c * L + lax.broadcasted_iota(I32, z.shape, 0)
    is_ig = col < SMALL_FG
    log_sig = jnp.minimum(z, 0.0) - jnp.log1p(jnp.exp(-jnp.abs(z)))
    gl = jnp.where(is_ig, z, log_sig)
    gl = jnp.where(tok < t_valid, gl, jnp.where(is_ig, NEG_INF, 0.0))
    gl_t = gl.T
    lane = lax.broadcasted_iota(I32, gl_t.shape, 1)
    cum_t = gl_t
    shift = 1
    while shift < L:
        cum_t = cum_t + jnp.where(lane >= shift, pltpu.roll(cum_t, shift, 1), 0.0)
        shift *= 2
    cum = cum_t.T

    row_i = lax.broadcasted_iota(I32, (L, L), 0)
    col_i = lax.broadcasted_iota(I32, (L, L), 1)
    causal = col_i <= row_i
    k_scale = A_HEAD_DIM ** -0.5

    for h in range(A_HEADS):
        sl = slice(h * A_HEAD_DIM, (h + 1) * A_HEAD_DIM)
        b_col = cum[:, SMALL_FG + h:SMALL_FG + h + 1]
        b_row = cum_t[SMALL_FG + h:SMALL_FG + h + 1, :]
        ig_col = gl[:, SMALL_IG + h:SMALL_IG + h + 1]
        ig_row = gl_t[SMALL_IG + h:SMALL_IG + h + 1, :]
        m_old = m_ref[h][0:1, 0:1]
        q = q_ref[:, sl]
        k = k_ref[:, sl]
        v = v_ref[:, sl]
        c_old = c_ref[h]
        n_old = n_ref[h]

        dmat = jnp.where(causal, b_col - b_row + ig_row, NEG_INF)
        a_inter = b_col + m_old
        m_j = jnp.maximum(a_inter, jnp.max(dmat, axis=1, keepdims=True))
        wmat = jnp.exp(dmat - m_j)
        inter = jnp.exp(a_inter - m_j)
        s = lax.dot_general(q, k, NT_DIMS, preferred_element_type=F32) * k_scale * wmat
        num = (jnp.dot(s.astype(BF16), v, preferred_element_type=F32)
               + jnp.dot(q, c_old.astype(BF16), preferred_element_type=F32) * inter)
        qn = jnp.sum(q.astype(F32) * n_old[0:1, :], axis=1, keepdims=True)
        den = jnp.sum(s, axis=1, keepdims=True) + inter * qn
        hh = num / jnp.maximum(jnp.abs(den), jnp.exp(-m_j))
        mu = jnp.mean(hh, axis=1, keepdims=True)
        var = jnp.mean(jnp.square(hh - mu), axis=1, keepdims=True)
        hn_ref[:, sl] = (hh - mu) * lax.rsqrt(var + HEAD_NORM_EPS)

        b_last = b_col[L - 1:L, :]
        g_col = b_last - b_col + ig_col
        g_row = b_last - b_row + ig_row
        m_new = jnp.maximum(b_last + m_old, jnp.max(g_row, axis=1, keepdims=True))
        decay = jnp.exp(b_last + m_old - m_new)
        w_col = jnp.exp(g_col - m_new) * k_scale
        kw = k.astype(F32) * w_col
        c_ref[h] = decay * c_old + jnp.dot(kw.T.astype(BF16), v, preferred_element_type=F32)
        n_ref[h] = decay * n_old + jnp.sum(kw, axis=0, keepdims=True)
        m_ref[h] = jnp.broadcast_to(m_new, (8, LANES))


def _mlstm(qkv, small, bg, c0, n0, m0, t_valid):
    b, t, _ = qkv.shape
    assert t % TILE == 0
    n0x = jnp.broadcast_to(n0[:, :, None, :], (b, A_HEADS, 8, A_HEAD_DIM))
    m0x = jnp.broadcast_to(m0[:, :, None, None], (b, A_HEADS, 8, LANES))
    qkv_spec = lambda j: pl.BlockSpec((None, TILE, A_WIDTH), lambda bi, ci: (bi, ci, j))
    st_c = pl.BlockSpec((None, A_HEADS, A_HEAD_DIM, A_HEAD_DIM), lambda bi, ci: (bi, 0, 0, 0))
    st_n = pl.BlockSpec((None, A_HEADS, 8, A_HEAD_DIM), lambda bi, ci: (bi, 0, 0, 0))
    st_m = pl.BlockSpec((None, A_HEADS, 8, LANES), lambda bi, ci: (bi, 0, 0, 0))
    hn, c_new, n_new, m_new = pl.pallas_call(
        functools.partial(_mlstm_kernel, t_valid=t_valid),
        grid=(b, t // TILE),
        in_specs=[qkv_spec(0), qkv_spec(1), qkv_spec(2),
                  pl.BlockSpec((None, TILE, LANES), lambda bi, ci: (bi, ci, 0)),
                  pl.BlockSpec((1, LANES), lambda bi, ci: (0, 0)),
                  st_c, st_n, st_m],
        out_specs=[pl.BlockSpec((None, TILE, A_WIDTH), lambda bi, ci: (bi, ci, 0)),
                   st_c, st_n, st_m],
        out_shape=[jax.ShapeDtypeStruct((b, t, A_WIDTH), F32),
                   jax.ShapeDtypeStruct((b, A_HEADS, A_HEAD_DIM, A_HEAD_DIM), F32),
                   jax.ShapeDtypeStruct((b, A_HEADS, 8, A_HEAD_DIM), F32),
                   jax.ShapeDtypeStruct((b, A_HEADS, 8, LANES), F32)],
        compiler_params=_cparams("parallel", "arbitrary"),
        name="mlstm",
    )(qkv, qkv, qkv, small, bg, c0, n0x, m0x)
    return hn, c_new, n_new[:, :, 0, :], m_new[:, :, 0, 0]


def _t5_bucket(rel):
    half = N_BUCKETS // 2
    max_exact = half // 2
    ret = jnp.where(rel > 0, half, 0)
    n = jnp.abs(rel)
    nf = jnp.maximum(n, 1).astype(F32)
    large = max_exact + (jnp.log(nf / max_exact) / math.log(MAX_DISTANCE / max_exact)
                         * (half - max_exact)).astype(I32)
    large = jnp.minimum(large, half - 1)
    return ret + jnp.where(n < max_exact, n, large)


def _bias_kernel(rb_ref, out_ref):
    kk = lax.broadcasted_iota(I32, (TILE, TILE), 0)
    qq = lax.broadcasted_iota(I32, (TILE, TILE), 1)
    for ti, off in enumerate((0, -TILE, -2 * TILE)):
        bucket = _t5_bucket(kk + off - qq)
        for h in range(B_HEADS):
            acc = jnp.zeros((TILE, TILE), F32)
            for bkt in range(N_BUCKETS):
                acc = jnp.where(bucket == bkt, rb_ref[bkt, h], acc)
            out_ref[h, ti] = acc * LOG2E


def _bias_tiles(rel_bias):
    return pl.pallas_call(
        _bias_kernel,
        in_specs=[pl.BlockSpec(memory_space=pltpu.SMEM)],
        out_specs=pl.BlockSpec(memory_space=pltpu.VMEM),
        out_shape=jax.ShapeDtypeStruct((B_HEADS, 3, TILE, TILE), F32),
        compiler_params=pltpu.CompilerParams(vmem_limit_bytes=VMEM_LIMIT_BYTES),
        name="bias_tiles",
    )(rel_bias.astype(F32))


def _dsa_kernel(q_ref, qi_ref, sm_ref, k_ref, v_ref, ki_ref, bias_ref, o_ref,
                keys_ref, madd_ref, lg_ref, acc_ref, bm_ref, m_ref, l_ref, thr_ref, jsel_ref,
                *, pos0, s_valid, k_sel):
    tq = q_ref.shape[0]
    tk = TILE
    t0 = pos0 + pl.program_id(1) * tq
    nkb = t0 // tk + 1
    qpos = t0 + lax.broadcasted_iota(I32, (1, tq), 1)
    limit = jnp.minimum((qpos // CHUNK + 1) * CHUNK, s_valid)
    blk_iota = lax.broadcasted_iota(I32, (tk, tq), 0)
    idx_bits = int(k_ref.shape[0]).bit_length()

    def rows(kb):
        return pl.ds(pl.multiple_of(kb * tk, tk), tk)

    w_t = sm_ref[...].T
    w_heads = [w_t[SMALL_WI + h:SMALL_WI + h + 1, :] * (IDX_HEADS ** -0.5) * (IDX_DIM ** -0.5)
               for h in range(IDX_HEADS)]

    def score_body(kb, carry):
        ki = ki_ref[rows(kb), :]
        sc = jnp.zeros((tk, tq), F32)
        for h in range(IDX_HEADS):
            qh = qi_ref[:, h * IDX_DIM:(h + 1) * IDX_DIM]
            d = lax.dot_general(ki, qh, NT_DIMS, preferred_element_type=F32)
            sc = sc + w_heads[h] * jnp.maximum(d, 0.0)
        sc = jnp.where(sc == 0.0, 0.0, sc)
        bits = lax.bitcast_convert_type(sc, I32)
        key = bits ^ ((bits >> 31) & 0x7FFFFFFF)
        adm = (kb * tk + blk_iota) < limit
        keys_ref[rows(kb), :] = jnp.where(adm, key, INT_MIN)
        return carry

    lax.fori_loop(0, nkb, score_body, 0)

    thr_ref[...] = jnp.full(thr_ref.shape, INT_MIN, I32)
    jsel_ref[...] = jnp.full(jsel_ref.shape, 2 ** 30, I32)

    def count(pred):
        def body(kb, acc):
            ones = jnp.where(pred(keys_ref[rows(kb), :], kb * tk + blk_iota), 1.0, 0.0)
            parts = [ones[r * 8:(r + 1) * 8, :] for r in range(tk // 8)]
            while len(parts) > 1:
                parts = [parts[i] + parts[i + 1] for i in range(0, len(parts), 2)]
            return acc + parts[0]
        acc = lax.fori_loop(0, nkb, body, jnp.zeros((8, tq), F32))
        return jnp.sum(acc, axis=0, keepdims=True)

    @pl.when(nkb * tk > k_sel)
    def _():
        kf = float(k_sel)
        zero = jnp.zeros((1, tq), I32)
        cnt = count(lambda blk, _: blk >= zero)
        thr = jnp.where(cnt >= kf, zero, jnp.full((1, tq), INT_MIN, I32))

        def bit_body(i, thr):
            cand = thr + jnp.left_shift(jnp.int32(1), 30 - i)
            cnt = count(lambda blk, _: blk >= cand)
            return jnp.where(cnt >= kf, cand, thr)

        thr = lax.fori_loop(0, 31, bit_body, thr)
        thr_ref[...] = jnp.broadcast_to(thr, thr_ref.shape)
        n_gt = count(lambda blk, _: blk > thr)
        n_ge = count(lambda blk, _: blk >= thr)
        need = kf - n_gt
        split = jnp.max(jnp.where(n_ge - n_gt > need, 1.0, 0.0)) > 0.0

        @pl.when(split)
        def _():
            def jbit_body(i, j0):
                cand = j0 + jnp.left_shift(jnp.int32(1), idx_bits - 1 - i)
                f = count(lambda blk, idx: (blk == thr) & (idx < cand))
                return jnp.where(f < need, cand, j0)

            j0 = lax.fori_loop(0, idx_bits, jbit_body, jnp.zeros((1, tq), I32))
            jsel_ref[...] = jnp.broadcast_to(j0 + 1, jsel_ref.shape)

    thr = thr_ref[0:1, :]
    jsel = jsel_ref[0:1, :]

    def mask_body(kb, carry):
        blk = keys_ref[rows(kb), :]
        idx = kb * tk + blk_iota
        sel = ((blk > thr) | ((blk == thr) & (idx < jsel))) & (idx < limit)
        madd_ref[rows(kb), :] = jnp.where(sel, 0.0, NEG_INF)
        return carry

    lax.fori_loop(0, nkb, mask_body, 0)

    scale = B_HEAD_DIM ** -0.5
    m_ref[...] = jnp.full(m_ref.shape, NEG_INF, F32)
    l_ref[...] = jnp.zeros(l_ref.shape, F32)
    acc_ref[...] = jnp.zeros(acc_ref.shape, F32)

    def attn_body(kb, carry):
        tile_id = jnp.minimum(nkb - 1 - kb, 2)
        for h in range(B_HEADS):
            sl = slice(h * B_HEAD_DIM, (h + 1) * B_HEAD_DIM)
            lt = lax.dot_general(k_ref[rows(kb), sl], q_ref[:, sl], NT_DIMS, preferred_element_type=F32)
            lt = lt * (scale * LOG2E) + bias_ref[h, tile_id] + madd_ref[rows(kb), :]
            lg_ref[h] = lt
            bm_ref[h:h + 1, :] = jnp.max(lt, axis=0, keepdims=True)
        for h in range(B_HEADS):
            sl = slice(h * B_HEAD_DIM, (h + 1) * B_HEAD_DIM)
            m_old = m_ref[h:h + 1, :]
            m_new = jnp.maximum(m_old, bm_ref[h:h + 1, :])
            m_safe = jnp.where(m_new == NEG_INF, 0.0, m_new)
            alpha = jnp.exp2(m_old - m_safe)
            p = jnp.exp2(lg_ref[h] - m_safe)
            l_ref[h:h + 1, :] = alpha * l_ref[h:h + 1, :] + jnp.sum(p, axis=0, keepdims=True)
            pv = lax.dot_general(v_ref[rows(kb), sl], p.astype(BF16), TN_DIMS,
                                 preferred_element_type=F32)
            acc_ref[h] = alpha * acc_ref[h] + pv
            m_ref[h:h + 1, :] = m_new
        return carry

    lax.fori_loop(0, nkb, attn_body, 0)
    for h in range(B_HEADS):
        sl = slice(h * B_HEAD_DIM, (h + 1) * B_HEAD_DIM)
        o_ref[:, sl] = (acc_ref[h] / l_ref[h:h + 1, :]).T


def _dsa(q, qi, small, k, v, ki, bias, pos0, s_valid, k_sel):
    b, tq_all, n_wide = q.shape
    s_pad = k.shape[1]
    assert tq_all % TILE == 0 and s_pad % TILE == 0 and pos0 % TILE == 0
    assert pos0 + tq_all <= s_pad and n_wide % B_WIDTH == 0
    q_block = n_wide // B_WIDTH - 1
    qspec = lambda w: pl.BlockSpec((None, TILE, w), lambda bi, ji: (bi, ji, 0))
    kspec = lambda w: pl.BlockSpec((None, s_pad, w), lambda bi, ji: (bi, 0, 0))
    return pl.pallas_call(
        functools.partial(_dsa_kernel, pos0=pos0, s_valid=s_valid, k_sel=k_sel),
        grid=(b, tq_all // TILE),
        in_specs=[pl.BlockSpec((None, TILE, B_WIDTH), lambda bi, ji: (bi, ji, q_block)),
                  qspec(IDX_HEADS * IDX_DIM), qspec(LANES),
                  kspec(B_WIDTH), kspec(B_WIDTH), kspec(IDX_DIM),
                  pl.BlockSpec((B_HEADS, 3, TILE, TILE), lambda bi, ji: (0, 0, 0, 0))],
        out_specs=qspec(B_WIDTH),
        out_shape=jax.ShapeDtypeStruct((b, tq_all, B_WIDTH), F32),
        scratch_shapes=[pltpu.VMEM((s_pad, TILE), I32),
                        pltpu.VMEM((s_pad, TILE), F32),
                        pltpu.VMEM((B_HEADS, TILE, TILE), F32),
                        pltpu.VMEM((B_HEADS, B_HEAD_DIM, TILE), F32),
                        pltpu.VMEM((B_HEADS, TILE), F32),
                        pltpu.VMEM((B_HEADS, TILE), F32),
                        pltpu.VMEM((B_HEADS, TILE), F32),
                        pltpu.VMEM((8, TILE), I32),
                        pltpu.VMEM((8, TILE), I32)],
        compiler_params=_cparams("parallel", "arbitrary"),
        name="dsa",
    )(q, qi, small, k, v, ki, bias)


def _merge_kernel(x_ref, hn_ref, ob_ref, wg_ref, woa_ref, wob_ref, wo_ref, vec_ref, y_ref):
    x = x_ref[...]
    xb = x.astype(BF16)
    gate = lambda i: jnp.dot(xb, wg_ref[i], preferred_element_type=F32)
    hn = hn_ref[...] * vec_ref[0:1, :]
    branch_a = (hn * jax.nn.sigmoid(gate(0))) * jax.nn.silu(gate(1))
    branch_b = ob_ref[...] * jax.nn.silu(gate(2))
    mixed = (jax.nn.sigmoid(gate(3)) * jnp.dot(branch_a.astype(BF16), woa_ref[...], preferred_element_type=F32)
             + jax.nn.sigmoid(gate(4)) * jnp.dot(branch_b.astype(BF16), wob_ref[...], preferred_element_type=F32))
    y = DN_ALPHA * x + jnp.dot(mixed.astype(BF16), wo_ref[...], preferred_element_type=F32)
    mu = jnp.mean(y, axis=1, keepdims=True)
    var = jnp.mean(jnp.square(y - mu), axis=1, keepdims=True)
    y_ref[...] = (y - mu) * lax.rsqrt(var + LN_EPS) * vec_ref[1:2, :] + vec_ref[2:3, :]


def _merge(x2d, hn2d, ob2d, wg, woa, wob, wo, vec):
    m, d = x2d.shape
    tm = min(MERGE_TM, m)
    assert m % tm == 0
    row = pl.BlockSpec((tm, d), lambda i: (i, 0))
    const = lambda shape: pl.BlockSpec(shape, lambda i: (0,) * len(shape))
    return pl.pallas_call(
        _merge_kernel,
        grid=(m // tm,),
        in_specs=[row, row, row, const(wg.shape), const(woa.shape), const(wob.shape),
                  const(wo.shape), const(vec.shape)],
        out_specs=row,
        out_shape=jax.ShapeDtypeStruct((m, d), F32),
        compiler_params=_cparams("parallel"),
        name="merge",
    )(x2d, hn2d, ob2d, wg, woa, wob, wo, vec)


def _split_weights(w_in, b_gates):
    cuts = [0]
    for c in COL_SIZES:
        cuts.append(cuts[-1] + c)
    cols = [w_in[:, cuts[i]:cuts[i + 1]] for i in range(len(COL_SIZES))]
    (qa, ka, va, oa, za, ia, fa, qb, kb, vb, zb, qi, ki, wi, ga, gb) = cols
    pad = lambda w, n: jnp.pad(w, ((0, 0), (0, n - w.shape[1])))
    small = jnp.concatenate([pad(ia, SMALL_FG - SMALL_IG), pad(fa, SMALL_WI - SMALL_FG),
                             pad(wi, LANES - SMALL_WI)], axis=1)
    bg = jnp.concatenate([jnp.pad(b_gates[:A_HEADS], (0, SMALL_FG - SMALL_IG - A_HEADS)),
                          jnp.pad(b_gates[A_HEADS:], (0, LANES - SMALL_FG - A_HEADS))])[None, :]
    c16 = lambda w: w.astype(BF16)
    return dict(wide=c16(jnp.concatenate([qa, ka, va, qb], axis=1)), kb=c16(kb), vb=c16(vb),
                narrow=c16(jnp.concatenate([qi, small, ki], axis=1)), bg=bg.astype(F32),
                gates=jnp.stack([c16(oa), c16(za), c16(zb), c16(ga), c16(gb)]))


def _pad_rows(a, n):
    return jnp.pad(a, ((0, 0), (0, n - a.shape[1]), (0, 0)))


def _group(x, w, consts, state, cache):
    b, t, d = x.shape
    x2d = x.reshape(b * t, d)
    n_wide = w["wide"].shape[1]
    n_qi = IDX_HEADS * IDX_DIM
    (wide,) = _project(x2d, w["wide"], [(0, n_wide, BF16)], "proj_wide")
    kb32, kb16 = _project(x2d, w["kb"], [(0, B_WIDTH, F32), (0, B_WIDTH, BF16)], "proj_kb")
    vb32, vb16 = _project(x2d, w["vb"], [(0, B_WIDTH, F32), (0, B_WIDTH, BF16)], "proj_vb")
    qi, small, ki32, ki16 = _project(
        x2d, w["narrow"],
        [(0, n_qi, BF16), (n_qi, LANES, F32), (n_qi + LANES, IDX_DIM, F32), (n_qi + LANES, IDX_DIM, BF16)],
        "proj_narrow")
    r3 = lambda a: a.reshape(b, t, a.shape[-1])
    wide, kb16, vb16, qi, ki16, small = map(r3, (wide, kb16, vb16, qi, ki16, small))

    t_pad = -(-t // TILE) * TILE
    wide = _pad_rows(wide, t_pad)
    c0, n0, m0 = state
    hn, c_new, n_new, m_new = _mlstm(wide, _pad_rows(small, t_pad), w["bg"], c0, n0, m0, t_valid=t)

    if cache is None:
        pos0, keys_k, keys_v, keys_i = 0, kb16, vb16, ki16
    else:
        cache_k, cache_v, cache_i = cache
        pos0 = cache_k.shape[1]
        flat = lambda a: a.reshape(b, pos0, -1).astype(BF16)
        keys_k = jnp.concatenate([flat(cache_k), kb16], axis=1)
        keys_v = jnp.concatenate([flat(cache_v), vb16], axis=1)
        keys_i = jnp.concatenate([flat(cache_i), ki16], axis=1)
    s_valid = pos0 + t
    s_pad = pos0 + t_pad
    k_sel = min(TOPK_MAX, s_valid // 4)
    ob = _dsa(wide, _pad_rows(qi, t_pad), _pad_rows(small, t_pad),
              _pad_rows(keys_k, s_pad), _pad_rows(keys_v, s_pad), _pad_rows(keys_i, s_pad),
              consts["bias"], pos0=pos0, s_valid=s_valid, k_sel=k_sel)

    y = _merge(x2d, hn[:, :t].reshape(b * t, A_WIDTH), ob[:, :t].reshape(b * t, B_WIDTH),
               w["gates"], consts["woa"], consts["wob"], consts["wo"], consts["vec"])
    return (y.reshape(b, t, d), kb32.reshape(b, t, B_HEADS, B_HEAD_DIM),
            vb32.reshape(b, t, B_HEADS, B_HEAD_DIM), ki32.reshape(b, t, IDX_DIM),
            c_new, n_new, m_new)


def kernel(x_prompt, x_sample, cache_k, cache_v, cache_idx_k, state_C, state_n, state_m,
           w_in, b_gates, a_norm_g, w_out_a, w_out_b, w_o, rel_bias, ln_g, ln_b):
    w = _split_weights(w_in, b_gates)
    consts = dict(bias=_bias_tiles(rel_bias), woa=w_out_a.astype(BF16), wob=w_out_b.astype(BF16),
                  wo=w_o.astype(BF16),
                  vec=jnp.pad(jnp.stack([a_norm_g, ln_g, ln_b]).astype(F32), ((0, 5), (0, 0))))
    bp = x_prompt.shape[0]
    zero_state = (jnp.zeros((bp, A_HEADS, A_HEAD_DIM, A_HEAD_DIM), F32),
                  jnp.zeros((bp, A_HEADS, A_HEAD_DIM), F32), jnp.zeros((bp, A_HEADS), F32))
    y_p, k_p, v_p, i_p, c_p, n_p, m_p = _group(x_prompt, w, consts, zero_state, None)
    y_s, k_s, v_s, i_s, c_s, n_s, m_s = _group(
        x_sample, w, consts,
        (state_C.astype(F32), state_n.astype(F32), state_m.astype(F32)),
        (cache_k, cache_v, cache_idx_k))
    return (y_p, y_s, k_p, v_p, i_p, c_p, n_p, m_p, k_s, v_s, i_s, c_s, n_s, m_s)
```

```python
import functools
import math

import jax
import jax.numpy as jnp
from jax import lax
from jax.experimental import pallas as pl
from jax.experimental.pallas import tpu as pltpu

F32 = jnp.float32
BF16 = jnp.bfloat16
I32 = jnp.int32

D_MODEL = 1024
DEPTH = 1
CHUNK = 64
A_HEADS = 4
A_HEAD_DIM = 256
A_WIDTH = A_HEADS * A_HEAD_DIM
B_HEADS = 8
B_HEAD_DIM = 128
B_WIDTH = B_HEADS * B_HEAD_DIM
IDX_HEADS = 8
IDX_DIM = 64
TOPK_MAX = 256
N_BUCKETS = 32
MAX_DISTANCE = 128
LN_EPS = 1e-5
HEAD_NORM_EPS = 1e-6
DN_ALPHA = (2 * DEPTH) ** 0.25

COL_SIZES = (A_WIDTH, A_WIDTH, A_WIDTH, A_WIDTH, A_WIDTH, A_HEADS, A_HEADS,
             B_WIDTH, B_WIDTH, B_WIDTH, B_WIDTH, IDX_HEADS * IDX_DIM, IDX_DIM, IDX_HEADS,
             D_MODEL, D_MODEL)

LANES = 128
MXU_DIM = 256
VMEM_LIMIT_BYTES = 56 * 1024 * 1024

TILE = MXU_DIM
PROJ_TM = 1024
PROJ_TN = 1024
MERGE_TM = 256

SMALL_IG = 0
SMALL_FG = 8
SMALL_WI = 16

LOG2E = math.log2(math.e)
INT_MIN = -2 ** 31
NEG_INF = float("-inf")
NT_DIMS = (((1,), (1,)), ((), ()))
TN_DIMS = (((0,), (0,)), ((), ()))


def _cparams(*sem):
    return pltpu.CompilerParams(dimension_semantics=sem, vmem_limit_bytes=VMEM_LIMIT_BYTES)


def _proj_kernel(x_ref, w_ref, *o_refs, col_slices):
    acc = jnp.dot(x_ref[...].astype(BF16), w_ref[...], preferred_element_type=F32)
    for o_ref, (c0, cw) in zip(o_refs, col_slices):
        o_ref[...] = acc[:, c0:c0 + cw].reshape(o_ref.shape).astype(o_ref.dtype)


def _project(x2d, w, outs, name, heads=None):
    m, k = x2d.shape
    n = w.shape[1]
    tm = min(PROJ_TM, m)
    full = all((c0, cw) == (0, n) for c0, cw, _ in outs)
    tn = min(PROJ_TN, n) if full else n
    assert m % tm == 0 and n % tn == 0
    col_slices = tuple((0, tn) if full else (c0, cw) for c0, cw, _ in outs)
    out_specs, out_shape = [], []
    for (c0, cw), (_, _, dt) in zip(col_slices, outs):
        if heads is not None and dt == F32 and cw == heads[0] * heads[1] and not (full and n != cw):
            out_specs.append(pl.BlockSpec((tm,) + heads, lambda i, j: (i, 0, 0)))
            out_shape.append(jax.ShapeDtypeStruct((m,) + heads, dt))
        else:
            out_specs.append(pl.BlockSpec((tm, cw), (lambda i, j: (i, j)) if full else (lambda i, j: (i, 0))))
            out_shape.append(jax.ShapeDtypeStruct((m, n if full else cw), dt))
    return pl.pallas_call(
        functools.partial(_proj_kernel, col_slices=col_slices),
        grid=(m // tm, n // tn),
        in_specs=[pl.BlockSpec((tm, k), lambda i, j: (i, 0)),
                  pl.BlockSpec((k, tn), lambda i, j: (0, j))],
        out_specs=out_specs,
        out_shape=out_shape,
        compiler_params=_cparams("parallel", "arbitrary"),
        name=name,
    )(x2d, w)


def _mlstm_kernel(q_ref, k_ref, v_ref, sm_ref, bg_ref, c0_ref, n0_ref, m0_ref,
                  hn_ref, c_ref, n_ref, m_ref, *, t_valid):
    L = q_ref.shape[0]
    c = pl.program_id(1)

    @pl.when(c == 0)
    def _():
        c_ref[...] = c0_ref[...]
        n_ref[...] = n0_ref[...]
        m_ref[...] = m0_ref[...]

    z = sm_ref[...] + bg_ref[...]
    col = lax.broadcasted_iota(I32, z.shape, 1)
    tok = c * L + lax.broadcasted_iota(I32, z.shape, 0)
    is_ig = col < SMALL_FG
    log_sig = jnp.minimum(z, 0.0) - jnp.log1p(jnp.exp(-jnp.abs(z)))
    gl = jnp.where(is_ig, z, log_sig)
    gl = jnp.where(tok < t_valid, gl, jnp.where(is_ig, NEG_INF, 0.0))
    gl_t = gl.T
    fg_t = gl_t[SMALL_FG:SMALL_FG + 8, :]
    lane = lax.broadcasted_iota(I32, fg_t.shape, 1)
    shift = 1
    while shift < L:
        fg_t = fg_t + jnp.where(lane >= shift, pltpu.roll(fg_t, shift, 1), 0.0)
        shift *= 2
    cum_t = jnp.concatenate([gl_t[:SMALL_FG], fg_t, gl_t[SMALL_FG + 8:]], axis=0)
    cum = cum_t.T

    k_scale = A_HEAD_DIM ** -0.5
    causal = lax.broadcasted_iota(I32, (L, L), 1) <= lax.broadcasted_iota(I32, (L, L), 0)

    for h in range(A_HEADS):
        sl = slice(h * A_HEAD_DIM, (h + 1) * A_HEAD_DIM)
        b_col = cum[:, SMALL_FG + h:SMALL_FG + h + 1]
        b_row = cum_t[SMALL_FG + h:SMALL_FG + h + 1, :]
        ig_col = gl[:, SMALL_IG + h:SMALL_IG + h + 1]
        ig_row = gl_t[SMALL_IG + h:SMALL_IG + h + 1, :]
        m_old = m_ref[h][0:1, 0:1]
        q = q_ref[:, sl]
        k = k_ref[:, sl]
        v = v_ref[:, sl]
        c_old = c_ref[h]
        n_old = n_ref[h]

        dmat = jnp.where(causal, b_col - b_row + ig_row, NEG_INF)
        a_inter = b_col + m_old
        m_j = jnp.maximum(a_inter, jnp.max(dmat, axis=1, keepdims=True))
        wmat = jnp.exp(dmat - m_j)
        inter = jnp.exp(a_inter - m_j)
        s = lax.dot_general(q, k, NT_DIMS, preferred_element_type=F32) * k_scale * wmat
        num = (jnp.dot(s.astype(BF16), v, preferred_element_type=F32)
               + jnp.dot(q, c_old.astype(BF16), preferred_element_type=F32) * inter)
        qn = jnp.sum(q.astype(F32) * n_old[0:1, :], axis=1, keepdims=True)
        den = jnp.sum(s, axis=1, keepdims=True) + inter * qn
        hh = num / jnp.maximum(jnp.abs(den), jnp.exp(-m_j))
        mu = jnp.mean(hh, axis=1, keepdims=True)
        var = jnp.mean(jnp.square(hh - mu), axis=1, keepdims=True)
        hn_ref[:, sl] = (hh - mu) * lax.rsqrt(var + HEAD_NORM_EPS)

        b_last = b_col[L - 1:L, :]
        g_col = b_last - b_col + ig_col
        g_row = b_last - b_row + ig_row
        m_new = jnp.maximum(b_last + m_old, jnp.max(g_row, axis=1, keepdims=True))
        decay = jnp.exp(b_last + m_old - m_new)
        w_col = jnp.exp(g_col - m_new) * k_scale
        kw = k.astype(F32) * w_col
        c_ref[h] = decay * c_old + jnp.dot(kw.T.astype(BF16), v, preferred_element_type=F32)
        n_ref[h] = decay * n_old + jnp.sum(kw, axis=0, keepdims=True)
        m_ref[h] = jnp.broadcast_to(m_new, (8, LANES))


def _mlstm(qkv, small, bg, c0, n0, m0, t_valid):
    b, t, _ = qkv.shape
    assert t % TILE == 0
    n0x = jnp.broadcast_to(n0[:, :, None, :], (b, A_HEADS, 8, A_HEAD_DIM))
    m0x = jnp.broadcast_to(m0[:, :, None, None], (b, A_HEADS, 8, LANES))
    qkv_spec = lambda j: pl.BlockSpec((None, TILE, A_WIDTH), lambda bi, ci: (bi, ci, j))
    st_c = pl.BlockSpec((None, A_HEADS, A_HEAD_DIM, A_HEAD_DIM), lambda bi, ci: (bi, 0, 0, 0))
    st_n = pl.BlockSpec((None, A_HEADS, 8, A_HEAD_DIM), lambda bi, ci: (bi, 0, 0, 0))
    st_m = pl.BlockSpec((None, A_HEADS, 8, LANES), lambda bi, ci: (bi, 0, 0, 0))
    hn, c_new, n_new, m_new = pl.pallas_call(
        functools.partial(_mlstm_kernel, t_valid=t_valid),
        grid=(b, t // TILE),
        in_specs=[qkv_spec(0), qkv_spec(1), qkv_spec(2),
                  pl.BlockSpec((None, TILE, LANES), lambda bi, ci: (bi, ci, 0)),
                  pl.BlockSpec((1, LANES), lambda bi, ci: (0, 0)),
                  st_c, st_n, st_m],
        out_specs=[pl.BlockSpec((None, TILE, A_WIDTH), lambda bi, ci: (bi, ci, 0)),
                   st_c, st_n, st_m],
        out_shape=[jax.ShapeDtypeStruct((b, t, A_WIDTH), F32),
                   jax.ShapeDtypeStruct((b, A_HEADS, A_HEAD_DIM, A_HEAD_DIM), F32),
                   jax.ShapeDtypeStruct((b, A_HEADS, 8, A_HEAD_DIM), F32),
                   jax.ShapeDtypeStruct((b, A_HEADS, 8, LANES), F32)],
        compiler_params=_cparams("parallel", "arbitrary"),
        name="mlstm",
    )(qkv, qkv, qkv, small, bg, c0, n0x, m0x)
    return hn, c_new, n_new[:, :, 0, :], m_new[:, :, 0, 0]


def _t5_bucket(rel):
    half = N_BUCKETS // 2
    max_exact = half // 2
    ret = jnp.where(rel > 0, half, 0)
    n = jnp.abs(rel)
    nf = jnp.maximum(n, 1).astype(F32)
    large = max_exact + (jnp.log(nf / max_exact) / math.log(MAX_DISTANCE / max_exact)
                         * (half - max_exact)).astype(I32)
    large = jnp.minimum(large, half - 1)
    return ret + jnp.where(n < max_exact, n, large)


def _bias_kernel(rb_ref, out_ref):
    kk = lax.broadcasted_iota(I32, (TILE, TILE), 0)
    qq = lax.broadcasted_iota(I32, (TILE, TILE), 1)
    for ti, off in enumerate((0, -TILE, -2 * TILE)):
        bucket = _t5_bucket(kk + off - qq)
        for h in range(B_HEADS):
            acc = jnp.zeros((TILE, TILE), F32)
            for bkt in range(N_BUCKETS):
                acc = jnp.where(bucket == bkt, rb_ref[bkt, h], acc)
            out_ref[h, ti] = acc * LOG2E


def _bias_tiles(rel_bias):
    return pl.pallas_call(
        _bias_kernel,
        in_specs=[pl.BlockSpec(memory_space=pltpu.SMEM)],
        out_specs=pl.BlockSpec(memory_space=pltpu.VMEM),
        out_shape=jax.ShapeDtypeStruct((B_HEADS, 3, TILE, TILE), F32),
        compiler_params=pltpu.CompilerParams(vmem_limit_bytes=VMEM_LIMIT_BYTES),
        name="bias_tiles",
    )(rel_bias.astype(F32))


def _dsa_kernel(q_ref, qi_ref, sm_ref, k_ref, v_ref, ki_ref, bias_ref, o_ref,
                keys_ref, madd_ref, lg_ref, acc_ref, bm_ref, m_ref, l_ref, thr_ref, jsel_ref,
                *, pos0, s_valid, k_sel):
    tq = q_ref.shape[0]
    tk = TILE
    t0 = pos0 + pl.program_id(1) * tq
    nkb = t0 // tk + 1
    qpos = t0 + lax.broadcasted_iota(I32, (1, tq), 1)
    limit = jnp.minimum((qpos // CHUNK + 1) * CHUNK, s_valid)
    blk_iota = lax.broadcasted_iota(I32, (tk, tq), 0)
    idx_bits = int(k_ref.shape[0]).bit_length()

    def rows(kb):
        return pl.ds(pl.multiple_of(kb * tk, tk), tk)

    w_t = sm_ref[...].T
    w_heads = [w_t[SMALL_WI + h:SMALL_WI + h + 1, :] * (IDX_HEADS ** -0.5) * (IDX_DIM ** -0.5)
               for h in range(IDX_HEADS)]

    def score_body(kb, carry):
        ki = ki_ref[rows(kb), :]
        sc = jnp.zeros((tk, tq), F32)
        for h in range(IDX_HEADS):
            qh = qi_ref[:, h * IDX_DIM:(h + 1) * IDX_DIM]
            d = lax.dot_general(ki, qh, NT_DIMS, preferred_element_type=F32)
            sc = sc + w_heads[h] * jnp.maximum(d, 0.0)
        sc = jnp.where(sc == 0.0, 0.0, sc)
        bits = lax.bitcast_convert_type(sc, I32)
        key = bits ^ ((bits >> 31) & 0x7FFFFFFF)
        adm = (kb * tk + blk_iota) < limit
        keys_ref[rows(kb), :] = jnp.where(adm, key, INT_MIN)
        return carry

    lax.fori_loop(0, nkb, score_body, 0)

    thr_ref[...] = jnp.full(thr_ref.shape, INT_MIN, I32)
    jsel_ref[...] = jnp.full(jsel_ref.shape, 2 ** 30, I32)

    def count(pred):
        def body(kb, acc):
            ones = jnp.where(pred(keys_ref[rows(kb), :], kb * tk + blk_iota), 1.0, 0.0)
            parts = [ones[r * 8:(r + 1) * 8, :] for r in range(tk // 8)]
            while len(parts) > 1:
                parts = [parts[i] + parts[i + 1] for i in range(0, len(parts), 2)]
            return acc + parts[0]
        acc = lax.fori_loop(0, nkb, body, jnp.zeros((8, tq), F32))
        return jnp.sum(acc, axis=0, keepdims=True)

    @pl.when(nkb * tk > k_sel)
    def _():
        kf = float(k_sel)
        zero = jnp.zeros((1, tq), I32)
        cnt = count(lambda blk, _: blk >= zero)
        thr = jnp.where(cnt >= kf, zero, jnp.full((1, tq), INT_MIN, I32))

        def bit_body(i, thr):
            cand = thr + jnp.left_shift(jnp.int32(1), 30 - i)
            cnt = count(lambda blk, _: blk >= cand)
            return jnp.where(cnt >= kf, cand, thr)

        thr = lax.fori_loop(0, 31, bit_body, thr)
        thr_ref[...] = jnp.broadcast_to(thr, thr_ref.shape)
        n_gt = count(lambda blk, _: blk > thr)
        n_ge = count(lambda blk, _: blk >= thr)
        need = kf - n_gt
        split = jnp.max(jnp.where(n_ge - n_gt > need, 1.0, 0.0)) > 0.0

        @pl.when(split)
        def _():
            def jbit_body(i, j0):
                cand = j0 + jnp.left_shift(jnp.int32(1), idx_bits - 1 - i)
                f = count(lambda blk, idx: (blk == thr) & (idx < cand))
                return jnp.where(f < need, cand, j0)

            j0 = lax.fori_loop(0, idx_bits, jbit_body, jnp.zeros((1, tq), I32))
            jsel_ref[...] = jnp.broadcast_to(j0 + 1, jsel_ref.shape)

    thr = thr_ref[0:1, :]
    jsel = jsel_ref[0:1, :]

    def mask_body(kb, carry):
        blk = keys_ref[rows(kb), :]
        idx = kb * tk + blk_iota
        sel = ((blk > thr) | ((blk == thr) & (idx < jsel))) & (idx < limit)
        madd_ref[rows(kb), :] = jnp.where(sel, 0.0, NEG_INF)
        return carry

    lax.fori_loop(0, nkb, mask_body, 0)

    scale = B_HEAD_DIM ** -0.5
    m_ref[...] = jnp.full(m_ref.shape, NEG_INF, F32)
    l_ref[...] = jnp.zeros(l_ref.shape, F32)
    acc_ref[...] = jnp.zeros(acc_ref.shape, F32)

    def attn_body(kb, carry, near):
        for h in range(B_HEADS):
            sl = slice(h * B_HEAD_DIM, (h + 1) * B_HEAD_DIM)
            lt = lax.dot_general(k_ref[rows(kb), sl], q_ref[:, sl], NT_DIMS, preferred_element_type=F32)
            lt = lt * (scale * LOG2E) + madd_ref[rows(kb), :]
            if near:
                lt = lt + bias_ref[h, nkb - 1 - kb]
            lg_ref[h] = lt
            bm_ref[h:h + 1, :] = jnp.max(lt, axis=0, keepdims=True)
        for h in range(B_HEADS):
            sl = slice(h * B_HEAD_DIM, (h + 1) * B_HEAD_DIM)
            m_old = m_ref[h:h + 1, :]
            m_new = jnp.maximum(m_old, bm_ref[h:h + 1, :])
            m_safe = jnp.where(m_new == NEG_INF, 0.0, m_new)
            alpha = jnp.exp2(m_old - m_safe)
            p = jnp.exp2(lg_ref[h] - m_safe)
            l_ref[h:h + 1, :] = alpha * l_ref[h:h + 1, :] + jnp.sum(p, axis=0, keepdims=True)
            pv = lax.dot_general(v_ref[rows(kb), sl], p.astype(BF16), TN_DIMS,
                                 preferred_element_type=F32)
            acc_ref[h] = alpha * acc_ref[h] + pv
            m_ref[h:h + 1, :] = m_new
        return carry

    n_far = jnp.maximum(nkb - 2, 0)
    lax.fori_loop(0, n_far, functools.partial(attn_body, near=False), 0)
    for h in range(B_HEADS):
        m_ref[h:h + 1, :] = m_ref[h:h + 1, :] + bias_ref[h, 2, 0:1, :]
    lax.fori_loop(n_far, nkb, functools.partial(attn_body, near=True), 0)
    for h in range(B_HEADS):
        sl = slice(h * B_HEAD_DIM, (h + 1) * B_HEAD_DIM)
        o_ref[:, sl] = (acc_ref[h] / l_ref[h:h + 1, :]).T


def _dsa(q, qi, small, k, v, ki, bias, pos0, s_valid, k_sel):
    b, tq_all, n_wide = q.shape
    s_pad = k.shape[1]
    assert tq_all % TILE == 0 and s_pad % TILE == 0 and pos0 % TILE == 0
    assert pos0 + tq_all <= s_pad and n_wide % B_WIDTH == 0
    q_block = n_wide // B_WIDTH - 1
    qspec = lambda w: pl.BlockSpec((None, TILE, w), lambda bi, ji: (bi, ji, 0))
    kspec = lambda w: pl.BlockSpec((None, s_pad, w), lambda bi, ji: (bi, 0, 0))
    return pl.pallas_call(
        functools.partial(_dsa_kernel, pos0=pos0, s_valid=s_valid, k_sel=k_sel),
        grid=(b, tq_all // TILE),
        in_specs=[pl.BlockSpec((None, TILE, B_WIDTH), lambda bi, ji: (bi, ji, q_block)),
                  qspec(IDX_HEADS * IDX_DIM), qspec(LANES),
                  kspec(B_WIDTH), kspec(B_WIDTH), kspec(IDX_DIM),
                  pl.BlockSpec((B_HEADS, 3, TILE, TILE), lambda bi, ji: (0, 0, 0, 0))],
        out_specs=qspec(B_WIDTH),
        out_shape=jax.ShapeDtypeStruct((b, tq_all, B_WIDTH), F32),
        scratch_shapes=[pltpu.VMEM((s_pad, TILE), I32),
                        pltpu.VMEM((s_pad, TILE), F32),
                        pltpu.VMEM((B_HEADS, TILE, TILE), F32),
                        pltpu.VMEM((B_HEADS, B_HEAD_DIM, TILE), F32),
                        pltpu.VMEM((B_HEADS, TILE), F32),
                        pltpu.VMEM((B_HEADS, TILE), F32),
                        pltpu.VMEM((B_HEADS, TILE), F32),
                        pltpu.VMEM((8, TILE), I32),
                        pltpu.VMEM((8, TILE), I32)],
        compiler_params=_cparams("parallel", "arbitrary"),
        name="dsa",
    )(q, qi, small, k, v, ki, bias)


def _merge_kernel(x_ref, hn_ref, ob_ref, wg_ref, woa_ref, wob_ref, wo_ref, vec_ref, y_ref):
    x = x_ref[...]
    xb = x.astype(BF16)
    gate = lambda i: jnp.dot(xb, wg_ref[i], preferred_element_type=F32)
    hn = hn_ref[...] * vec_ref[0:1, :]
    branch_a = (hn * jax.nn.sigmoid(gate(0))) * jax.nn.silu(gate(1))
    branch_b = ob_ref[...] * jax.nn.silu(gate(2))
    mixed = (jax.nn.sigmoid(gate(3)) * jnp.dot(branch_a.astype(BF16), woa_ref[...], preferred_element_type=F32)
             + jax.nn.sigmoid(gate(4)) * jnp.dot(branch_b.astype(BF16), wob_ref[...], preferred_element_type=F32))
    y = DN_ALPHA * x + jnp.dot(mixed.astype(BF16), wo_ref[...], preferred_element_type=F32)
    mu = jnp.mean(y, axis=1, keepdims=True)
    var = jnp.mean(jnp.square(y - mu), axis=1, keepdims=True)
    y_ref[...] = (y - mu) * lax.rsqrt(var + LN_EPS) * vec_ref[1:2, :] + vec_ref[2:3, :]


def _merge(x2d, hn2d, ob2d, wg, woa, wob, wo, vec):
    m, d = x2d.shape
    tm = min(MERGE_TM, m)
    assert m % tm == 0
    row = pl.BlockSpec((tm, d), lambda i: (i, 0))
    const = lambda shape: pl.BlockSpec(shape, lambda i: (0,) * len(shape))
    return pl.pallas_call(
        _merge_kernel,
        grid=(m // tm,),
        in_specs=[row, row, row, const(wg.shape), const(woa.shape), const(wob.shape),
                  const(wo.shape), const(vec.shape)],
        out_specs=row,
        out_shape=jax.ShapeDtypeStruct((m, d), F32),
        compiler_params=_cparams("parallel"),
        name="merge",
    )(x2d, hn2d, ob2d, wg, woa, wob, wo, vec)


def _split_weights(w_in, b_gates):
    cuts = [0]
    for c in COL_SIZES:
        cuts.append(cuts[-1] + c)
    cols = [w_in[:, cuts[i]:cuts[i + 1]] for i in range(len(COL_SIZES))]
    (qa, ka, va, oa, za, ia, fa, qb, kb, vb, zb, qi, ki, wi, ga, gb) = cols
    pad = lambda w, n: jnp.pad(w, ((0, 0), (0, n - w.shape[1])))
    small = jnp.concatenate([pad(ia, SMALL_FG - SMALL_IG), pad(fa, SMALL_WI - SMALL_FG),
                             pad(wi, LANES - SMALL_WI)], axis=1)
    bg = jnp.concatenate([jnp.pad(b_gates[:A_HEADS], (0, SMALL_FG - SMALL_IG - A_HEADS)),
                          jnp.pad(b_gates[A_HEADS:], (0, LANES - SMALL_FG - A_HEADS))])[None, :]
    c16 = lambda w: w.astype(BF16)
    return dict(wide=c16(jnp.concatenate([qa, ka, va, qb], axis=1)), kb=c16(kb), vb=c16(vb),
                narrow=c16(jnp.concatenate([qi, small, ki], axis=1)), bg=bg.astype(F32),
                gates=jnp.stack([c16(oa), c16(za), c16(zb), c16(ga), c16(gb)]))


def _pad_rows(a, n):
    return jnp.pad(a, ((0, 0), (0, n - a.shape[1]), (0, 0)))


def _group(x, w, consts, state, cache):
    b, t, d = x.shape
    x2d = x.reshape(b * t, d)
    n_wide = w["wide"].shape[1]
    n_qi = IDX_HEADS * IDX_DIM
    (wide,) = _project(x2d, w["wide"], [(0, n_wide, BF16)], "proj_wide")
    bh = (B_HEADS, B_HEAD_DIM)
    kb32, kb16 = _project(x2d, w["kb"], [(0, B_WIDTH, F32), (0, B_WIDTH, BF16)], "proj_kb", heads=bh)
    vb32, vb16 = _project(x2d, w["vb"], [(0, B_WIDTH, F32), (0, B_WIDTH, BF16)], "proj_vb", heads=bh)
    qi, small, ki32, ki16 = _project(
        x2d, w["narrow"],
        [(0, n_qi, BF16), (n_qi, LANES, F32), (n_qi + LANES, IDX_DIM, F32), (n_qi + LANES, IDX_DIM, BF16)],
        "proj_narrow")
    r3 = lambda a: a.reshape(b, t, a.shape[-1])
    wide, kb16, vb16, qi, ki16, small = map(r3, (wide, kb16, vb16, qi, ki16, small))

    t_pad = -(-t // TILE) * TILE
    wide = _pad_rows(wide, t_pad)
    c0, n0, m0 = state
    hn, c_new, n_new, m_new = _mlstm(wide, _pad_rows(small, t_pad), w["bg"], c0, n0, m0, t_valid=t)

    if cache is None:
        pos0, keys_k, keys_v, keys_i = 0, kb16, vb16, ki16
    else:
        cache_k, cache_v, cache_i = cache
        pos0 = cache_k.shape[1]
        flat = lambda a: a.reshape(b, pos0, -1).astype(BF16)
        keys_k = jnp.concatenate([flat(cache_k), kb16], axis=1)
        keys_v = jnp.concatenate([flat(cache_v), vb16], axis=1)
        keys_i = jnp.concatenate([flat(cache_i), ki16], axis=1)
    s_valid = pos0 + t
    s_pad = pos0 + t_pad
    k_sel = min(TOPK_MAX, s_valid // 4)
    ob = _dsa(wide, _pad_rows(qi, t_pad), _pad_rows(small, t_pad),
              _pad_rows(keys_k, s_pad), _pad_rows(keys_v, s_pad), _pad_rows(keys_i, s_pad),
              consts["bias"], pos0=pos0, s_valid=s_valid, k_sel=k_sel)

    y = _merge(x2d, hn[:, :t].reshape(b * t, A_WIDTH), ob[:, :t].reshape(b * t, B_WIDTH),
               w["gates"], consts["woa"], consts["wob"], consts["wo"], consts["vec"])
    return (y.reshape(b, t, d), kb32.reshape(b, t, B_HEADS, B_HEAD_DIM),
            vb32.reshape(b, t, B_HEADS, B_HEAD_DIM), ki32.reshape(b, t, IDX_DIM),
            c_new, n_new, m_new)


def kernel(x_prompt, x_sample, cache_k, cache_v, cache_idx_k, state_C, state_n, state_m,
           w_in, b_gates, a_norm_g, w_out_a, w_out_b, w_o, rel_bias, ln_g, ln_b):
    w = _split_weights(w_in, b_gates)
    consts = dict(bias=_bias_tiles(rel_bias), woa=w_out_a.astype(BF16), wob=w_out_b.astype(BF16),
                  wo=w_o.astype(BF16),
                  vec=jnp.pad(jnp.stack([a_norm_g, ln_g, ln_b]).astype(F32), ((0, 5), (0, 0))))
    bp = x_prompt.shape[0]
    zero_state = (jnp.zeros((bp, A_HEADS, A_HEAD_DIM, A_HEAD_DIM), F32),
                  jnp.zeros((bp, A_HEADS, A_HEAD_DIM), F32), jnp.zeros((bp, A_HEADS), F32))
    y_p, k_p, v_p, i_p, c_p, n_p, m_p = _group(x_prompt, w, consts, zero_state, None)
    y_s, k_s, v_s, i_s, c_s, n_s, m_s = _group(
        x_sample, w, consts,
        (state_C.astype(F32), state_n.astype(F32), state_m.astype(F32)),
        (cache_k, cache_v, cache_idx_k))
    return (y_p, y_s, k_p, v_p, i_p, c_p, n_p, m_p, k_s, v_s, i_s, c_s, n_s, m_s)
```

```python
import functools
import math

import jax
import jax.numpy as jnp
from jax import lax
from jax.experimental import pallas as pl
from jax.experimental.pallas import tpu as pltpu

F32 = jnp.float32
BF16 = jnp.bfloat16
I32 = jnp.int32

D_MODEL = 1024
DEPTH = 1
CHUNK = 64
A_HEADS = 4
A_HEAD_DIM = 256
A_WIDTH = A_HEADS * A_HEAD_DIM
B_HEADS = 8
B_HEAD_DIM = 128
B_WIDTH = B_HEADS * B_HEAD_DIM
IDX_HEADS = 8
IDX_DIM = 64
TOPK_MAX = 256
N_BUCKETS = 32
MAX_DISTANCE = 128
LN_EPS = 1e-5
HEAD_NORM_EPS = 1e-6
DN_ALPHA = (2 * DEPTH) ** 0.25

COL_SIZES = (A_WIDTH, A_WIDTH, A_WIDTH, A_WIDTH, A_WIDTH, A_HEADS, A_HEADS,
             B_WIDTH, B_WIDTH, B_WIDTH, B_WIDTH, IDX_HEADS * IDX_DIM, IDX_DIM, IDX_HEADS,
             D_MODEL, D_MODEL)

LANES = 128
MXU_DIM = 256
VMEM_LIMIT_BYTES = 56 * 1024 * 1024

TILE = MXU_DIM
PROJ_TM = 1024
PROJ_TN = 1024
MERGE_TM = 256

SMALL_IG = 0
SMALL_FG = 8
SMALL_WI = 16

LOG2E = math.log2(math.e)
INT_MIN = -2 ** 31
F32_MIN_NORMAL = 2.0 ** -126
KEY_NEG_INF = (0xFF800000 ^ 0x7FFFFFFF) - 2 ** 32
KEY_POS_INF = 0x7F800000
KEY_MIN_NORMAL = 0x00800000
NEG_INF = float("-inf")
NT_DIMS = (((1,), (1,)), ((), ()))
TN_DIMS = (((0,), (0,)), ((), ()))


def _cparams(*sem):
    return pltpu.CompilerParams(dimension_semantics=sem, vmem_limit_bytes=VMEM_LIMIT_BYTES)


def _proj_kernel(x_ref, w_ref, *o_refs, col_slices):
    acc = jnp.dot(x_ref[...].astype(BF16), w_ref[...], preferred_element_type=F32)
    for o_ref, (c0, cw) in zip(o_refs, col_slices):
        o_ref[...] = acc[:, c0:c0 + cw].reshape(o_ref.shape).astype(o_ref.dtype)


def _project(x2d, w, outs, name, heads=None):
    m, k = x2d.shape
    n = w.shape[1]
    tm = min(PROJ_TM, m)
    full = all((c0, cw) == (0, n) for c0, cw, _ in outs)
    tn = min(PROJ_TN, n) if full else n
    assert m % tm == 0 and n % tn == 0
    col_slices = tuple((0, tn) if full else (c0, cw) for c0, cw, _ in outs)
    out_specs, out_shape = [], []
    for (c0, cw), (_, _, dt) in zip(col_slices, outs):
        if heads is not None and dt == F32 and cw == heads[0] * heads[1] and not (full and n != cw):
            out_specs.append(pl.BlockSpec((tm,) + heads, lambda i, j: (i, 0, 0)))
            out_shape.append(jax.ShapeDtypeStruct((m,) + heads, dt))
        else:
            out_specs.append(pl.BlockSpec((tm, cw), (lambda i, j: (i, j)) if full else (lambda i, j: (i, 0))))
            out_shape.append(jax.ShapeDtypeStruct((m, n if full else cw), dt))
    return pl.pallas_call(
        functools.partial(_proj_kernel, col_slices=col_slices),
        grid=(m // tm, n // tn),
        in_specs=[pl.BlockSpec((tm, k), lambda i, j: (i, 0)),
                  pl.BlockSpec((k, tn), lambda i, j: (0, j))],
        out_specs=out_specs,
        out_shape=out_shape,
        compiler_params=_cparams("parallel", "arbitrary"),
        name=name,
    )(x2d, w)


def _mlstm_kernel(q_ref, k_ref, v_ref, sm_ref, bg_ref, c0_ref, n0_ref, m0_ref,
                  hn_ref, c_ref, n_ref, m_ref, *, t_valid):
    L = q_ref.shape[0]
    c = pl.program_id(1)

    @pl.when(c == 0)
    def _():
        c_ref[...] = c0_ref[...]
        n_ref[...] = n0_ref[...]
        m_ref[...] = m0_ref[...]

    z = sm_ref[...] + bg_ref[...]
    col = lax.broadcasted_iota(I32, z.shape, 1)
    tok = c * L + lax.broadcasted_iota(I32, z.shape, 0)
    is_ig = col < SMALL_FG
    log_sig = jnp.minimum(z, 0.0) - jnp.log1p(jnp.exp(-jnp.abs(z)))
    gl = jnp.where(is_ig, z, log_sig)
    gl = jnp.where(tok < t_valid, gl, jnp.where(is_ig, NEG_INF, 0.0))
    gl_t = gl.T
    fg_t = gl_t[SMALL_FG:SMALL_FG + 8, :]
    lane = lax.broadcasted_iota(I32, fg_t.shape, 1)
    shift = 1
    while shift < L:
        fg_t = fg_t + jnp.where(lane >= shift, pltpu.roll(fg_t, shift, 1), 0.0)
        shift *= 2
    cum_t = jnp.concatenate([gl_t[:SMALL_FG], fg_t, gl_t[SMALL_FG + 8:]], axis=0)
    cum = cum_t.T

    k_scale = A_HEAD_DIM ** -0.5
    causal = lax.broadcasted_iota(I32, (L, L), 1) <= lax.broadcasted_iota(I32, (L, L), 0)

    for h in range(A_HEADS):
        sl = slice(h * A_HEAD_DIM, (h + 1) * A_HEAD_DIM)
        b_col = cum[:, SMALL_FG + h:SMALL_FG + h + 1]
        b_row = cum_t[SMALL_FG + h:SMALL_FG + h + 1, :]
        ig_col = gl[:, SMALL_IG + h:SMALL_IG + h + 1]
        ig_row = gl_t[SMALL_IG + h:SMALL_IG + h + 1, :]
        m_old = m_ref[h][0:1, 0:1]
        q = q_ref[:, sl]
        k = k_ref[:, sl]
        v = v_ref[:, sl]
        c_old = c_ref[h]
        n_old = n_ref[h]

        dmat = jnp.where(causal, b_col - b_row + ig_row, NEG_INF)
        a_inter = b_col + m_old
        m_j = jnp.maximum(a_inter, jnp.max(dmat, axis=1, keepdims=True))
        wmat = jnp.exp(dmat - m_j)
        inter = jnp.exp(a_inter - m_j)
        s = lax.dot_general(q, k, NT_DIMS, preferred_element_type=F32) * k_scale * wmat
        num = (jnp.dot(s.astype(BF16), v, preferred_element_type=F32)
               + jnp.dot(q, c_old.astype(BF16), preferred_element_type=F32) * inter)
        qn = jnp.sum(q.astype(F32) * n_old[0:1, :], axis=1, keepdims=True)
        den = jnp.sum(s, axis=1, keepdims=True) + inter * qn
        hh = num / jnp.maximum(jnp.abs(den), jnp.exp(-m_j))
        mu = jnp.mean(hh, axis=1, keepdims=True)
        var = jnp.mean(jnp.square(hh - mu), axis=1, keepdims=True)
        hn_ref[:, sl] = (hh - mu) * lax.rsqrt(var + HEAD_NORM_EPS)

        b_last = b_col[L - 1:L, :]
        g_col = b_last - b_col + ig_col
        g_row = b_last - b_row + ig_row
        m_new = jnp.maximum(b_last + m_old, jnp.max(g_row, axis=1, keepdims=True))
        decay = jnp.exp(b_last + m_old - m_new)
        w_col = jnp.exp(g_col - m_new) * k_scale
        kw = k.astype(F32) * w_col
        c_ref[h] = decay * c_old + jnp.dot(kw.T.astype(BF16), v, preferred_element_type=F32)
        n_ref[h] = decay * n_old + jnp.sum(kw, axis=0, keepdims=True)
        m_ref[h] = jnp.broadcast_to(m_new, (8, LANES))


def _mlstm(qkv, small, bg, c0, n0, m0, t_valid):
    b, t, _ = qkv.shape
    assert t % TILE == 0
    n0x = jnp.broadcast_to(n0[:, :, None, :], (b, A_HEADS, 8, A_HEAD_DIM))
    m0x = jnp.broadcast_to(m0[:, :, None, None], (b, A_HEADS, 8, LANES))
    qkv_spec = lambda j: pl.BlockSpec((None, TILE, A_WIDTH), lambda bi, ci: (bi, ci, j))
    st_c = pl.BlockSpec((None, A_HEADS, A_HEAD_DIM, A_HEAD_DIM), lambda bi, ci: (bi, 0, 0, 0))
    st_n = pl.BlockSpec((None, A_HEADS, 8, A_HEAD_DIM), lambda bi, ci: (bi, 0, 0, 0))
    st_m = pl.BlockSpec((None, A_HEADS, 8, LANES), lambda bi, ci: (bi, 0, 0, 0))
    hn, c_new, n_new, m_new = pl.pallas_call(
        functools.partial(_mlstm_kernel, t_valid=t_valid),
        grid=(b, t // TILE),
        in_specs=[qkv_spec(0), qkv_spec(1), qkv_spec(2),
                  pl.BlockSpec((None, TILE, LANES), lambda bi, ci: (bi, ci, 0)),
                  pl.BlockSpec((1, LANES), lambda bi, ci: (0, 0)),
                  st_c, st_n, st_m],
        out_specs=[pl.BlockSpec((None, TILE, A_WIDTH), lambda bi, ci: (bi, ci, 0)),
                   st_c, st_n, st_m],
        out_shape=[jax.ShapeDtypeStruct((b, t, A_WIDTH), F32),
                   jax.ShapeDtypeStruct((b, A_HEADS, A_HEAD_DIM, A_HEAD_DIM), F32),
                   jax.ShapeDtypeStruct((b, A_HEADS, 8, A_HEAD_DIM), F32),
                   jax.ShapeDtypeStruct((b, A_HEADS, 8, LANES), F32)],
        compiler_params=_cparams("parallel", "arbitrary"),
        name="mlstm",
    )(qkv, qkv, qkv, small, bg, c0, n0x, m0x)
    return hn, c_new, n_new[:, :, 0, :], m_new[:, :, 0, 0]


def _t5_bucket(rel):
    half = N_BUCKETS // 2
    max_exact = half // 2
    ret = jnp.where(rel > 0, half, 0)
    n = jnp.abs(rel)
    nf = jnp.maximum(n, 1).astype(F32)
    large = max_exact + (jnp.log(nf / max_exact) / math.log(MAX_DISTANCE / max_exact)
                         * (half - max_exact)).astype(I32)
    large = jnp.minimum(large, half - 1)
    return ret + jnp.where(n < max_exact, n, large)


def _bias_kernel(rb_ref, out_ref):
    kk = lax.broadcasted_iota(I32, (TILE, TILE), 0)
    qq = lax.broadcasted_iota(I32, (TILE, TILE), 1)
    for ti, off in enumerate((0, -TILE, -2 * TILE)):
        bucket = _t5_bucket(kk + off - qq)
        for h in range(B_HEADS):
            acc = jnp.zeros((TILE, TILE), F32)
            for bkt in range(N_BUCKETS):
                acc = jnp.where(bucket == bkt, rb_ref[bkt, h], acc)
            out_ref[h, ti] = acc * LOG2E


def _bias_tiles(rel_bias):
    return pl.pallas_call(
        _bias_kernel,
        in_specs=[pl.BlockSpec(memory_space=pltpu.SMEM)],
        out_specs=pl.BlockSpec(memory_space=pltpu.VMEM),
        out_shape=jax.ShapeDtypeStruct((B_HEADS, 3, TILE, TILE), F32),
        compiler_params=pltpu.CompilerParams(vmem_limit_bytes=VMEM_LIMIT_BYTES),
        name="bias_tiles",
    )(rel_bias.astype(F32))


def _dsa_kernel(q_ref, qi_ref, sm_ref, k_ref, v_ref, ki_ref, bias_ref, o_ref,
                keys_ref, top_ref, madd_ref, lg_ref, acc_ref, bm_ref, m_ref, l_ref, thr_ref, jsel_ref,
                *, pos0, s_valid, k_sel):
    tq = q_ref.shape[0]
    tk = TILE
    t0 = pos0 + pl.program_id(1) * tq
    nkb = t0 // tk + 1
    qpos = t0 + lax.broadcasted_iota(I32, (1, tq), 1)
    limit = jnp.minimum((qpos // CHUNK + 1) * CHUNK, s_valid)
    blk_iota = lax.broadcasted_iota(I32, (tk, tq), 0)
    idx_bits = int(k_ref.shape[0]).bit_length()

    def rows(kb):
        return pl.ds(pl.multiple_of(kb * tk, tk), tk)

    w_t = sm_ref[...].T
    w_heads = [w_t[SMALL_WI + h:SMALL_WI + h + 1, :] * (IDX_HEADS ** -0.5) * (IDX_DIM ** -0.5)
               for h in range(IDX_HEADS)]

    def score_body(kb, carry):
        ki = ki_ref[rows(kb), :]
        sc = jnp.zeros((tk, tq), F32)
        for h in range(IDX_HEADS):
            qh = qi_ref[:, h * IDX_DIM:(h + 1) * IDX_DIM]
            d = lax.dot_general(ki, qh, NT_DIMS, preferred_element_type=F32)
            sc = sc + w_heads[h] * jnp.maximum(d, 0.0)
        sc = jnp.where(jnp.abs(sc) < F32_MIN_NORMAL, 0.0, sc)
        bits = lax.bitcast_convert_type(sc, I32)
        key = bits ^ ((bits >> 31) & 0x7FFFFFFF)
        adm = (kb * tk + blk_iota) < limit
        keys_ref[rows(kb), :] = jnp.where(adm, key, KEY_NEG_INF)
        top = lax.bitcast_convert_type(bits & jnp.int32(-65536), F32)
        top_ref[rows(kb), :] = jnp.where(adm, top, NEG_INF).astype(BF16)
        return carry

    lax.fori_loop(0, nkb, score_body, 0)

    thr_ref[...] = jnp.full(thr_ref.shape, INT_MIN, I32)
    jsel_ref[...] = jnp.full(jsel_ref.shape, 2 ** 30, I32)

    def count_top(cand16):
        bits16 = cand16 ^ ((cand16 >> 15) & 0x7FFF)
        cand = lax.bitcast_convert_type(jnp.left_shift(bits16, 16), F32)
        cand = jnp.where((cand16 > 0) & (cand16 < KEY_MIN_NORMAL >> 16), F32_MIN_NORMAL, cand)
        cand = jnp.where(cand16 < KEY_NEG_INF >> 16, NEG_INF, cand)
        cand = jnp.where(cand16 > KEY_POS_INF >> 16, -NEG_INF, cand).astype(BF16)
        one, zero = jnp.ones((), BF16), jnp.zeros((), BF16)

        def body(kb, acc):
            ones = jnp.where(top_ref[rows(kb), :] >= cand, one, zero)
            parts = [ones[r * 16:(r + 1) * 16, :] for r in range(tk // 16)]
            while len(parts) > 1:
                parts = [parts[i] + parts[i + 1] for i in range(0, len(parts), 2)]
            return acc + parts[0].astype(F32)
        acc = lax.fori_loop(0, nkb, body, jnp.zeros((16, tq), F32))
        return jnp.sum(acc, axis=0, keepdims=True)

    def count(pred):
        def body(kb, acc):
            ones = jnp.where(pred(keys_ref[rows(kb), :], kb * tk + blk_iota), 1.0, 0.0)
            parts = [ones[r * 8:(r + 1) * 8, :] for r in range(tk // 8)]
            while len(parts) > 1:
                parts = [parts[i] + parts[i + 1] for i in range(0, len(parts), 2)]
            return acc + parts[0]
        acc = lax.fori_loop(0, nkb, body, jnp.zeros((8, tq), F32))
        return jnp.sum(acc, axis=0, keepdims=True)

    @pl.when(nkb * tk > k_sel)
    def _():
        kf = float(k_sel)
        zero = jnp.zeros((1, tq), I32)
        cnt = count_top(zero)
        top = jnp.where(cnt >= kf, zero, jnp.full((1, tq), -2 ** 15, I32))

        def top_body(i, top):
            cand = top + jnp.left_shift(jnp.int32(1), 14 - i)
            return jnp.where(count_top(cand) >= kf, cand, top)

        top = lax.fori_loop(0, 15, top_body, top)

        def bit_body(i, thr):
            cand = thr + jnp.left_shift(jnp.int32(1), 15 - i)
            cnt = count(lambda blk, _: blk >= cand)
            return jnp.where(cnt >= kf, cand, thr)

        thr = lax.fori_loop(0, 16, bit_body, jnp.left_shift(top, 16))
        thr_ref[...] = jnp.broadcast_to(thr, thr_ref.shape)
        n_gt = count(lambda blk, _: blk > thr)
        n_ge = count(lambda blk, _: blk >= thr)
        need = kf - n_gt
        split = jnp.max(jnp.where(n_ge - n_gt > need, 1.0, 0.0)) > 0.0

        @pl.when(split)
        def _():
            def jbit_body(i, j0):
                cand = j0 + jnp.left_shift(jnp.int32(1), idx_bits - 1 - i)
                f = count(lambda blk, idx: (blk == thr) & (idx < cand))
                return jnp.where(f < need, cand, j0)

            j0 = lax.fori_loop(0, idx_bits, jbit_body, jnp.zeros((1, tq), I32))
            jsel_ref[...] = jnp.broadcast_to(j0 + 1, jsel_ref.shape)

    thr = thr_ref[0:1, :]
    jsel = jsel_ref[0:1, :]

    def mask_body(kb, carry):
        blk = keys_ref[rows(kb), :]
        idx = kb * tk + blk_iota
        sel = ((blk > thr) | ((blk == thr) & (idx < jsel))) & (idx < limit)
        madd_ref[rows(kb), :] = jnp.where(sel, 0.0, NEG_INF)
        return carry

    lax.fori_loop(0, nkb, mask_body, 0)

    scale = B_HEAD_DIM ** -0.5
    m_ref[...] = jnp.full(m_ref.shape, NEG_INF, F32)
    l_ref[...] = jnp.zeros(l_ref.shape, F32)
    acc_ref[...] = jnp.zeros(acc_ref.shape, F32)

    def attn_body(kb, carry, near):
        for h in range(B_HEADS):
            sl = slice(h * B_HEAD_DIM, (h + 1) * B_HEAD_DIM)
            lt = lax.dot_general(k_ref[rows(kb), sl], q_ref[:, sl], NT_DIMS, preferred_element_type=F32)
            lt = lt * (scale * LOG2E) + madd_ref[rows(kb), :]
            if near:
                lt = lt + bias_ref[h, nkb - 1 - kb, :, :tq]
            lg_ref[h] = lt
            bm_ref[h:h + 1, :] = jnp.max(lt, axis=0, keepdims=True)
        for h in range(B_HEADS):
            sl = slice(h * B_HEAD_DIM, (h + 1) * B_HEAD_DIM)
            m_old = m_ref[h:h + 1, :]
            m_new = jnp.maximum(m_old, bm_ref[h:h + 1, :])
            m_safe = jnp.where(m_new == NEG_INF, 0.0, m_new)
            alpha = jnp.exp2(m_old - m_safe)
            p = jnp.exp2(lg_ref[h] - m_safe)
            l_ref[h:h + 1, :] = alpha * l_ref[h:h + 1, :] + jnp.sum(p, axis=0, keepdims=True)
            pv = lax.dot_general(v_ref[rows(kb), sl], p.astype(BF16), TN_DIMS,
                                 preferred_element_type=F32)
            acc_ref[h] = alpha * acc_ref[h] + pv
            m_ref[h:h + 1, :] = m_new
        return carry

    n_far = jnp.maximum(nkb - 2, 0)
    lax.fori_loop(0, n_far, functools.partial(attn_body, near=False), 0)
    for h in range(B_HEADS):
        m_ref[h:h + 1, :] = m_ref[h:h + 1, :] + bias_ref[h, 2, 0:1, :tq]
    lax.fori_loop(n_far, nkb, functools.partial(attn_body, near=True), 0)
    for h in range(B_HEADS):
        sl = slice(h * B_HEAD_DIM, (h + 1) * B_HEAD_DIM)
        o_ref[:, sl] = (acc_ref[h] / l_ref[h:h + 1, :]).T


def _dsa(q, qi, small, k, v, ki, bias, pos0, s_valid, k_sel, n_query):
    b, tq_all, n_wide = q.shape
    s_pad = k.shape[1]
    tq = LANES if n_query <= LANES else TILE
    n_tiles = -(-n_query // tq)
    assert s_pad % TILE == 0 and pos0 % TILE == 0 and (tq == TILE or n_tiles == 1)
    assert n_tiles * tq <= tq_all and pos0 + n_tiles * tq <= s_pad and n_wide % B_WIDTH == 0
    q_block = n_wide // B_WIDTH - 1
    qspec = lambda w: pl.BlockSpec((None, tq, w), lambda bi, ji: (bi, ji, 0))
    kspec = lambda w: pl.BlockSpec((None, s_pad, w), lambda bi, ji: (bi, 0, 0))
    return pl.pallas_call(
        functools.partial(_dsa_kernel, pos0=pos0, s_valid=s_valid, k_sel=k_sel),
        grid=(b, n_tiles),
        in_specs=[pl.BlockSpec((None, tq, B_WIDTH), lambda bi, ji: (bi, ji, q_block)),
                  qspec(IDX_HEADS * IDX_DIM), qspec(LANES),
                  kspec(B_WIDTH), kspec(B_WIDTH), kspec(IDX_DIM),
                  pl.BlockSpec((B_HEADS, 3, TILE, TILE), lambda bi, ji: (0, 0, 0, 0))],
        out_specs=qspec(B_WIDTH),
        out_shape=jax.ShapeDtypeStruct((b, n_tiles * tq, B_WIDTH), F32),
        scratch_shapes=[pltpu.VMEM((s_pad, tq), I32),
                        pltpu.VMEM((s_pad, tq), BF16),
                        pltpu.VMEM((s_pad, tq), F32),
                        pltpu.VMEM((B_HEADS, TILE, tq), F32),
                        pltpu.VMEM((B_HEADS, B_HEAD_DIM, tq), F32),
                        pltpu.VMEM((B_HEADS, tq), F32),
                        pltpu.VMEM((B_HEADS, tq), F32),
                        pltpu.VMEM((B_HEADS, tq), F32),
                        pltpu.VMEM((8, tq), I32),
                        pltpu.VMEM((8, tq), I32)],
        compiler_params=_cparams("parallel", "arbitrary"),
        name="dsa",
    )(q, qi, small, k, v, ki, bias)


def _pack_kernel(cache_ref, new_ref, o_ref, *, n_cache_blocks):
    j = pl.program_id(1)

    @pl.when(j < n_cache_blocks)
    def _():
        o_ref[...] = cache_ref[...].reshape(o_ref.shape).astype(o_ref.dtype)

    @pl.when(j >= n_cache_blocks)
    def _():
        o_ref[...] = new_ref[...]


def _pack_rows(cache, new16):
    b, p, hh, dd = cache.shape
    tn = new16.shape[1]
    assert p % TILE == 0 and tn % TILE == 0 and new16.shape[2] == hh * dd
    ncb = p // TILE
    return pl.pallas_call(
        functools.partial(_pack_kernel, n_cache_blocks=ncb),
        grid=(b, (p + tn) // TILE),
        in_specs=[pl.BlockSpec((None, TILE, hh, dd), lambda bi, ji: (bi, jnp.minimum(ji, ncb - 1), 0, 0)),
                  pl.BlockSpec((None, TILE, hh * dd), lambda bi, ji: (bi, jnp.maximum(ji - ncb, 0), 0))],
        out_specs=pl.BlockSpec((None, TILE, hh * dd), lambda bi, ji: (bi, ji, 0)),
        out_shape=jax.ShapeDtypeStruct((b, p + tn, hh * dd), BF16),
        compiler_params=_cparams("parallel", "arbitrary"),
        name="pack_rows",
    )(cache, new16)


def _merge_kernel(x_ref, hn_ref, ob_ref, wg_ref, woa_ref, wob_ref, wo_ref, vec_ref, y_ref):
    x = x_ref[...]
    xb = x.astype(BF16)
    gate = lambda i: jnp.dot(xb, wg_ref[i], preferred_element_type=F32)
    hn = hn_ref[...] * vec_ref[0:1, :]
    branch_a = (hn * jax.nn.sigmoid(gate(0))) * jax.nn.silu(gate(1))
    branch_b = ob_ref[...] * jax.nn.silu(gate(2))
    mixed = (jax.nn.sigmoid(gate(3)) * jnp.dot(branch_a.astype(BF16), woa_ref[...], preferred_element_type=F32)
             + jax.nn.sigmoid(gate(4)) * jnp.dot(branch_b.astype(BF16), wob_ref[...], preferred_element_type=F32))
    y = DN_ALPHA * x + jnp.dot(mixed.astype(BF16), wo_ref[...], preferred_element_type=F32)
    mu = jnp.mean(y, axis=1, keepdims=True)
    var = jnp.mean(jnp.square(y - mu), axis=1, keepdims=True)
    y_ref[...] = (y - mu) * lax.rsqrt(var + LN_EPS) * vec_ref[1:2, :] + vec_ref[2:3, :]


def _merge(x2d, hn2d, ob2d, wg, woa, wob, wo, vec):
    m, d = x2d.shape
    tm = min(MERGE_TM, m)
    assert m % tm == 0
    row = pl.BlockSpec((tm, d), lambda i: (i, 0))
    const = lambda shape: pl.BlockSpec(shape, lambda i: (0,) * len(shape))
    return pl.pallas_call(
        _merge_kernel,
        grid=(m // tm,),
        in_specs=[row, row, row, const(wg.shape), const(woa.shape), const(wob.shape),
                  const(wo.shape), const(vec.shape)],
        out_specs=row,
        out_shape=jax.ShapeDtypeStruct((m, d), F32),
        compiler_params=_cparams("parallel"),
        name="merge",
    )(x2d, hn2d, ob2d, wg, woa, wob, wo, vec)


def _split_weights(w_in, b_gates):
    cuts = [0]
    for c in COL_SIZES:
        cuts.append(cuts[-1] + c)
    cols = [w_in[:, cuts[i]:cuts[i + 1]] for i in range(len(COL_SIZES))]
    (qa, ka, va, oa, za, ia, fa, qb, kb, vb, zb, qi, ki, wi, ga, gb) = cols
    pad = lambda w, n: jnp.pad(w, ((0, 0), (0, n - w.shape[1])))
    small = jnp.concatenate([pad(ia, SMALL_FG - SMALL_IG), pad(fa, SMALL_WI - SMALL_FG),
                             pad(wi, LANES - SMALL_WI)], axis=1)
    bg = jnp.concatenate([jnp.pad(b_gates[:A_HEADS], (0, SMALL_FG - SMALL_IG - A_HEADS)),
                          jnp.pad(b_gates[A_HEADS:], (0, LANES - SMALL_FG - A_HEADS))])[None, :]
    c16 = lambda w: w.astype(BF16)
    return dict(wide=c16(jnp.concatenate([qa, ka, va, qb], axis=1)), kb=c16(kb), vb=c16(vb),
                narrow=c16(jnp.concatenate([qi, small, ki], axis=1)), bg=bg.astype(F32),
                gates=jnp.stack([c16(oa), c16(za), c16(zb), c16(ga), c16(gb)]))


def _pad_rows(a, n):
    return jnp.pad(a, ((0, 0), (0, n - a.shape[1]), (0, 0)))


def _group(x, w, consts, state, cache):
    b, t, d = x.shape
    x2d = x.reshape(b * t, d)
    n_wide = w["wide"].shape[1]
    n_qi = IDX_HEADS * IDX_DIM
    (wide,) = _project(x2d, w["wide"], [(0, n_wide, BF16)], "proj_wide")
    bh = (B_HEADS, B_HEAD_DIM)
    kb32, kb16 = _project(x2d, w["kb"], [(0, B_WIDTH, F32), (0, B_WIDTH, BF16)], "proj_kb", heads=bh)
    vb32, vb16 = _project(x2d, w["vb"], [(0, B_WIDTH, F32), (0, B_WIDTH, BF16)], "proj_vb", heads=bh)
    qi, small, ki32, ki16 = _project(
        x2d, w["narrow"],
        [(0, n_qi, BF16), (n_qi, LANES, F32), (n_qi + LANES, IDX_DIM, F32), (n_qi + LANES, IDX_DIM, BF16)],
        "proj_narrow")
    r3 = lambda a: a.reshape(b, t, a.shape[-1])
    wide, kb16, vb16, qi, ki16, small = map(r3, (wide, kb16, vb16, qi, ki16, small))

    t_pad = -(-t // TILE) * TILE
    wide = _pad_rows(wide, t_pad)
    c0, n0, m0 = state
    hn, c_new, n_new, m_new = _mlstm(wide, _pad_rows(small, t_pad), w["bg"], c0, n0, m0, t_valid=t)

    if cache is None:
        pos0, keys_k, keys_v, keys_i = 0, kb16, vb16, ki16
    else:
        cache_k, cache_v, cache_i = cache
        pos0 = cache_k.shape[1]
        keys_k = _pack_rows(cache_k, _pad_rows(kb16, t_pad))
        keys_v = _pack_rows(cache_v, _pad_rows(vb16, t_pad))
        keys_i = jnp.concatenate([cache_i.astype(BF16), ki16], axis=1)
    s_valid = pos0 + t
    s_pad = pos0 + t_pad
    k_sel = min(TOPK_MAX, s_valid // 4)
    ob = _dsa(wide, _pad_rows(qi, t_pad), _pad_rows(small, t_pad),
              _pad_rows(keys_k, s_pad), _pad_rows(keys_v, s_pad), _pad_rows(keys_i, s_pad),
              consts["bias"], pos0=pos0, s_valid=s_valid, k_sel=k_sel, n_query=t)

    y = _merge(x2d, hn[:, :t].reshape(b * t, A_WIDTH), ob[:, :t].reshape(b * t, B_WIDTH),
               w["gates"], consts["woa"], consts["wob"], consts["wo"], consts["vec"])
    return (y.reshape(b, t, d), kb32.reshape(b, t, B_HEADS, B_HEAD_DIM),
            vb32.reshape(b, t, B_HEADS, B_HEAD_DIM), ki32.reshape(b, t, IDX_DIM),
            c_new, n_new, m_new)


def kernel(x_prompt, x_sample, cache_k, cache_v, cache_idx_k, state_C, state_n, state_m,
           w_in, b_gates, a_norm_g, w_out_a, w_out_b, w_o, rel_bias, ln_g, ln_b):
    w = _split_weights(w_in, b_gates)
    consts = dict(bias=_bias_tiles(rel_bias), woa=w_out_a.astype(BF16), wob=w_out_b.astype(BF16),
                  wo=w_o.astype(BF16),
                  vec=jnp.pad(jnp.stack([a_norm_g, ln_g, ln_b]).astype(F32), ((0, 5), (0, 0))))
    bp = x_prompt.shape[0]
    zero_state = (jnp.zeros((bp, A_HEADS, A_HEAD_DIM, A_HEAD_DIM), F32),
                  jnp.zeros((bp, A_HEADS, A_HEAD_DIM), F32), jnp.zeros((bp, A_HEADS), F32))
    y_p, k_p, v_p, i_p, c_p, n_p, m_p = _group(x_prompt, w, consts, zero_state, None)
    y_s, k_s, v_s, i_s, c_s, n_s, m_s = _group(
        x_sample, w, consts,
        (state_C.astype(F32), state_n.astype(F32), state_m.astype(F32)),
        (cache_k, cache_v, cache_idx_k))
    return (y_p, y_s, k_p, v_p, i_p, c_p, n_p, m_p, k_s, v_s, i_s, c_s, n_s, m_s)
```

```python
import functools
import math

import jax
import jax.numpy as jnp
from jax import lax
from jax.experimental import pallas as pl
from jax.experimental.pallas import tpu as pltpu

F32 = jnp.float32
BF16 = jnp.bfloat16
I32 = jnp.int32

D_MODEL = 1024
DEPTH = 1
CHUNK = 64
A_HEADS = 4
A_HEAD_DIM = 256
A_WIDTH = A_HEADS * A_HEAD_DIM
B_HEADS = 8
B_HEAD_DIM = 128
B_WIDTH = B_HEADS * B_HEAD_DIM
IDX_HEADS = 8
IDX_DIM = 64
TOPK_MAX = 256
N_BUCKETS = 32
MAX_DISTANCE = 128
LN_EPS = 1e-5
HEAD_NORM_EPS = 1e-6
DN_ALPHA = (2 * DEPTH) ** 0.25

COL_SIZES = (A_WIDTH, A_WIDTH, A_WIDTH, A_WIDTH, A_WIDTH, A_HEADS, A_HEADS,
             B_WIDTH, B_WIDTH, B_WIDTH, B_WIDTH, IDX_HEADS * IDX_DIM, IDX_DIM, IDX_HEADS,
             D_MODEL, D_MODEL)

LANES = 128
MXU_DIM = 256
VMEM_LIMIT_BYTES = 56 * 1024 * 1024

TILE = MXU_DIM
PROJ_TM = 1024
PROJ_TN = 1024
MERGE_TM = 256

SMALL_IG = 0
SMALL_FG = 8
SMALL_WI = 16

LOG2E = math.log2(math.e)
INT_MIN = -2 ** 31
F32_MIN_NORMAL = 2.0 ** -126
KEY_NEG_INF = (0xFF800000 ^ 0x7FFFFFFF) - 2 ** 32
KEY_POS_INF = 0x7F800000
KEY_MIN_NORMAL = 0x00800000
NEG_INF = float("-inf")
NT_DIMS = (((1,), (1,)), ((), ()))
TN_DIMS = (((0,), (0,)), ((), ()))


def _cparams(*sem):
    return pltpu.CompilerParams(dimension_semantics=sem, vmem_limit_bytes=VMEM_LIMIT_BYTES)


def _proj_kernel(x_ref, w_ref, *o_refs, col_slices):
    acc = jnp.dot(x_ref[...].astype(BF16), w_ref[...], preferred_element_type=F32)
    for o_ref, (c0, cw) in zip(o_refs, col_slices):
        o_ref[...] = acc[:, c0:c0 + cw].reshape(o_ref.shape).astype(o_ref.dtype)


def _project(x2d, w, outs, name, heads=None):
    m, k = x2d.shape
    n = w.shape[1]
    tm = min(PROJ_TM, m)
    full = all((c0, cw) == (0, n) for c0, cw, _ in outs)
    tn = min(PROJ_TN, n) if full else n
    assert m % tm == 0 and n % tn == 0
    col_slices = tuple((0, tn) if full else (c0, cw) for c0, cw, _ in outs)
    out_specs, out_shape = [], []
    for (c0, cw), (_, _, dt) in zip(col_slices, outs):
        if heads is not None and dt == F32 and cw == heads[0] * heads[1] and not (full and n != cw):
            out_specs.append(pl.BlockSpec((tm,) + heads, lambda i, j: (i, 0, 0)))
            out_shape.append(jax.ShapeDtypeStruct((m,) + heads, dt))
        else:
            out_specs.append(pl.BlockSpec((tm, cw), (lambda i, j: (i, j)) if full else (lambda i, j: (i, 0))))
            out_shape.append(jax.ShapeDtypeStruct((m, n if full else cw), dt))
    return pl.pallas_call(
        functools.partial(_proj_kernel, col_slices=col_slices),
        grid=(m // tm, n // tn),
        in_specs=[pl.BlockSpec((tm, k), lambda i, j: (i, 0)),
                  pl.BlockSpec((k, tn), lambda i, j: (0, j))],
        out_specs=out_specs,
        out_shape=out_shape,
        compiler_params=_cparams("parallel", "arbitrary"),
        name=name,
    )(x2d, w)


def _gates_kernel(sm_ref, bg_ref, grow_ref, gcol_ref, *, t_valid):
    z = sm_ref[...] + bg_ref[...]
    col = lax.broadcasted_iota(I32, z.shape, 1)
    tok = lax.broadcasted_iota(I32, z.shape, 0)
    is_ig = col < SMALL_FG
    log_sig = jnp.minimum(z, 0.0) - jnp.log1p(jnp.exp(-jnp.abs(z)))
    gl = jnp.where(is_ig, z, log_sig)
    gl = jnp.where(tok < t_valid, gl, jnp.where(is_ig, NEG_INF, 0.0))
    gl_t = gl.T
    ig_t = gl_t[SMALL_IG:SMALL_IG + 8, :]
    fg_t = gl_t[SMALL_FG:SMALL_FG + 8, :]
    in_chunk = lax.broadcasted_iota(I32, fg_t.shape, 1) % TILE
    shift = 1
    while shift < TILE:
        fg_t = fg_t + jnp.where(in_chunk >= shift, pltpu.roll(fg_t, shift, 1), 0.0)
        shift *= 2
    rows = jnp.concatenate([ig_t - fg_t, fg_t], axis=0)
    grow_ref[...] = rows
    pad = jnp.zeros((LANES - rows.shape[0], rows.shape[1]), F32)
    gcol_ref[...] = jnp.concatenate([rows, pad], axis=0).T


def _gates(small, bg, t_valid):
    b, t, _ = small.shape
    return pl.pallas_call(
        functools.partial(_gates_kernel, t_valid=t_valid),
        grid=(b,),
        in_specs=[pl.BlockSpec((None, t, LANES), lambda bi: (bi, 0, 0)),
                  pl.BlockSpec((1, LANES), lambda bi: (0, 0))],
        out_specs=[pl.BlockSpec((None, 16, t), lambda bi: (bi, 0, 0)),
                   pl.BlockSpec((None, t, LANES), lambda bi: (bi, 0, 0))],
        out_shape=[jax.ShapeDtypeStruct((b, 16, t), F32), jax.ShapeDtypeStruct((b, t, LANES), F32)],
        compiler_params=_cparams("parallel"),
        name="gates",
    )(small, bg)


def _mlstm_kernel(q_ref, k_ref, v_ref, grow_ref, gcol_ref, c0_ref, n0_ref, m0_ref,
                  hn_ref, c_ref, n_ref, m_ref, ct_ref):
    L = q_ref.shape[0]
    c = pl.program_id(1)

    @pl.when(c == 0)
    def _():
        for h in range(A_HEADS):
            ct_ref[h] = c0_ref[h].T
        n_ref[...] = n0_ref[...]
        m_ref[...] = m0_ref[...]

    k_scale = A_HEAD_DIM ** -0.5
    causal_t = lax.broadcasted_iota(I32, (L, L), 0) <= lax.broadcasted_iota(I32, (L, L), 1)

    for h in range(A_HEADS):
        sl = slice(h * A_HEAD_DIM, (h + 1) * A_HEAD_DIM)
        b_row = grow_ref[SMALL_FG + h:SMALL_FG + h + 1, :]
        c_col = gcol_ref[:, SMALL_IG + h:SMALL_IG + h + 1]
        m_old = m_ref[h][0:1, 0:1]
        q = q_ref[:, sl]
        k = k_ref[:, sl]
        v = v_ref[:, sl]
        ct_old = ct_ref[h]
        n_old = n_ref[h]

        dmat_t = jnp.where(causal_t, b_row + c_col, NEG_INF)
        a_inter = b_row + m_old
        m_j = jnp.maximum(a_inter, jnp.max(dmat_t, axis=0, keepdims=True))
        inter = jnp.exp(a_inter - m_j)
        s_t = (lax.dot_general(k, q, NT_DIMS, preferred_element_type=F32) * k_scale
               * jnp.exp(dmat_t - m_j))
        num_t = (lax.dot_general(v, s_t.astype(BF16), TN_DIMS, preferred_element_type=F32)
                 + lax.dot_general(ct_old.astype(BF16), q, NT_DIMS, preferred_element_type=F32) * inter)
        n_rows = jnp.broadcast_to(n_old[0:1, :], (16, A_HEAD_DIM)).astype(BF16)
        qn = lax.dot_general(n_rows, q, NT_DIMS, preferred_element_type=F32)[0:1, :]
        den = jnp.sum(s_t, axis=0, keepdims=True) + inter * qn
        hh_t = num_t / jnp.maximum(jnp.abs(den), jnp.exp(-m_j))
        mu = jnp.mean(hh_t, axis=0, keepdims=True)
        var = jnp.mean(jnp.square(hh_t - mu), axis=0, keepdims=True)
        hn_ref[:, sl] = ((hh_t - mu) * lax.rsqrt(var + HEAD_NORM_EPS)).T

        b_last = b_row[:, L - 1:L]
        g_col = b_last + c_col
        m_new = jnp.maximum(b_last + m_old, jnp.max(g_col, axis=0, keepdims=True))
        decay = jnp.exp(b_last + m_old - m_new)
        w_col = jnp.exp(g_col - m_new) * k_scale
        kw = k.astype(F32) * w_col
        ct_ref[h] = decay * ct_old + lax.dot_general(v, kw.astype(BF16), TN_DIMS,
                                                     preferred_element_type=F32)
        n_ref[h] = decay * n_old + jnp.sum(kw, axis=0, keepdims=True)
        m_ref[h] = jnp.broadcast_to(m_new, (8, LANES))

    @pl.when(c == pl.num_programs(1) - 1)
    def _():
        for h in range(A_HEADS):
            c_ref[h] = ct_ref[h].T


def _mlstm(qkv, small, bg, c0, n0, m0, t_valid):
    b, t, _ = qkv.shape
    assert t % TILE == 0
    grow, gcol = _gates(small, bg, t_valid)
    n0x = jnp.broadcast_to(n0[:, :, None, :], (b, A_HEADS, 8, A_HEAD_DIM))
    m0x = jnp.broadcast_to(m0[:, :, None, None], (b, A_HEADS, 8, LANES))
    qkv_spec = lambda j: pl.BlockSpec((None, TILE, A_WIDTH), lambda bi, ci: (bi, ci, j))
    st_c = pl.BlockSpec((None, A_HEADS, A_HEAD_DIM, A_HEAD_DIM), lambda bi, ci: (bi, 0, 0, 0))
    st_n = pl.BlockSpec((None, A_HEADS, 8, A_HEAD_DIM), lambda bi, ci: (bi, 0, 0, 0))
    st_m = pl.BlockSpec((None, A_HEADS, 8, LANES), lambda bi, ci: (bi, 0, 0, 0))
    hn, c_new, n_new, m_new = pl.pallas_call(
        _mlstm_kernel,
        grid=(b, t // TILE),
        in_specs=[qkv_spec(0), qkv_spec(1), qkv_spec(2),
                  pl.BlockSpec((None, 16, TILE), lambda bi, ci: (bi, 0, ci)),
                  pl.BlockSpec((None, TILE, LANES), lambda bi, ci: (bi, ci, 0)),
                  st_c, st_n, st_m],
        out_specs=[pl.BlockSpec((None, TILE, A_WIDTH), lambda bi, ci: (bi, ci, 0)),
                   st_c, st_n, st_m],
        out_shape=[jax.ShapeDtypeStruct((b, t, A_WIDTH), F32),
                   jax.ShapeDtypeStruct((b, A_HEADS, A_HEAD_DIM, A_HEAD_DIM), F32),
                   jax.ShapeDtypeStruct((b, A_HEADS, 8, A_HEAD_DIM), F32),
                   jax.ShapeDtypeStruct((b, A_HEADS, 8, LANES), F32)],
        scratch_shapes=[pltpu.VMEM((A_HEADS, A_HEAD_DIM, A_HEAD_DIM), F32)],
        compiler_params=_cparams("parallel", "arbitrary"),
        name="mlstm",
    )(qkv, qkv, qkv, grow, gcol, c0, n0x, m0x)
    return hn, c_new, n_new[:, :, 0, :], m_new[:, :, 0, 0]


def _t5_bucket(rel):
    half = N_BUCKETS // 2
    max_exact = half // 2
    ret = jnp.where(rel > 0, half, 0)
    n = jnp.abs(rel)
    nf = jnp.maximum(n, 1).astype(F32)
    large = max_exact + (jnp.log(nf / max_exact) / math.log(MAX_DISTANCE / max_exact)
                         * (half - max_exact)).astype(I32)
    large = jnp.minimum(large, half - 1)
    return ret + jnp.where(n < max_exact, n, large)


def _bias_kernel(rb_ref, out_ref):
    kk = lax.broadcasted_iota(I32, (TILE, TILE), 0)
    qq = lax.broadcasted_iota(I32, (TILE, TILE), 1)
    for ti, off in enumerate((0, -TILE, -2 * TILE)):
        bucket = _t5_bucket(kk + off - qq)
        for h in range(B_HEADS):
            acc = jnp.zeros((TILE, TILE), F32)
            for bkt in range(N_BUCKETS):
                acc = jnp.where(bucket == bkt, rb_ref[bkt, h], acc)
            out_ref[h, ti] = acc * LOG2E


def _bias_tiles(rel_bias):
    return pl.pallas_call(
        _bias_kernel,
        in_specs=[pl.BlockSpec(memory_space=pltpu.SMEM)],
        out_specs=pl.BlockSpec(memory_space=pltpu.VMEM),
        out_shape=jax.ShapeDtypeStruct((B_HEADS, 3, TILE, TILE), F32),
        compiler_params=pltpu.CompilerParams(vmem_limit_bytes=VMEM_LIMIT_BYTES),
        name="bias_tiles",
    )(rel_bias.astype(F32))


def _dsa_kernel(q_ref, qi_ref, sm_ref, k_ref, v_ref, ki_ref, bias_ref, o_ref,
                keys_ref, top_ref, madd_ref, lg_ref, acc_ref, bm_ref, m_ref, l_ref, thr_ref, jsel_ref,
                *, pos0, s_valid, k_sel):
    tq = q_ref.shape[0]
    tk = TILE
    t0 = pos0 + pl.program_id(1) * tq
    nkb = t0 // tk + 1
    qpos = t0 + lax.broadcasted_iota(I32, (1, tq), 1)
    limit = jnp.minimum((qpos // CHUNK + 1) * CHUNK, s_valid)
    blk_iota = lax.broadcasted_iota(I32, (tk, tq), 0)
    idx_bits = int(k_ref.shape[0]).bit_length()

    def rows(kb):
        return pl.ds(pl.multiple_of(kb * tk, tk), tk)

    w_t = sm_ref[...].T
    w_heads = [w_t[SMALL_WI + h:SMALL_WI + h + 1, :] * (IDX_HEADS ** -0.5) * (IDX_DIM ** -0.5)
               for h in range(IDX_HEADS)]

    def score_body(kb, carry):
        ki = ki_ref[rows(kb), :]
        sc = jnp.zeros((tk, tq), F32)
        for h in range(IDX_HEADS):
            qh = qi_ref[:, h * IDX_DIM:(h + 1) * IDX_DIM]
            d = lax.dot_general(ki, qh, NT_DIMS, preferred_element_type=F32)
            sc = sc + w_heads[h] * jnp.maximum(d, 0.0)
        sc = jnp.where(jnp.abs(sc) < F32_MIN_NORMAL, 0.0, sc)
        bits = lax.bitcast_convert_type(sc, I32)
        key = bits ^ ((bits >> 31) & 0x7FFFFFFF)
        adm = (kb * tk + blk_iota) < limit
        keys_ref[rows(kb), :] = jnp.where(adm, key, KEY_NEG_INF)
        top = lax.bitcast_convert_type(bits & jnp.int32(-65536), F32)
        top_ref[rows(kb), :] = jnp.where(adm, top, NEG_INF).astype(BF16)
        return carry

    lax.fori_loop(0, nkb, score_body, 0)

    thr_ref[...] = jnp.full(thr_ref.shape, INT_MIN, I32)
    jsel_ref[...] = jnp.full(jsel_ref.shape, 2 ** 30, I32)

    def count_top(cand16):
        bits16 = cand16 ^ ((cand16 >> 15) & 0x7FFF)
        cand = lax.bitcast_convert_type(jnp.left_shift(bits16, 16), F32)
        cand = jnp.where((cand16 > 0) & (cand16 < KEY_MIN_NORMAL >> 16), F32_MIN_NORMAL, cand)
        cand = jnp.where(cand16 < KEY_NEG_INF >> 16, NEG_INF, cand)
        cand = jnp.where(cand16 > KEY_POS_INF >> 16, -NEG_INF, cand).astype(BF16)
        one, zero = jnp.ones((), BF16), jnp.zeros((), BF16)

        def body(kb, acc):
            ones = jnp.where(top_ref[rows(kb), :] >= cand, one, zero)
            parts = [ones[r * 16:(r + 1) * 16, :] for r in range(tk // 16)]
            while len(parts) > 1:
                parts = [parts[i] + parts[i + 1] for i in range(0, len(parts), 2)]
            return acc + parts[0].astype(F32)
        acc = lax.fori_loop(0, nkb, body, jnp.zeros((16, tq), F32))
        return jnp.sum(acc, axis=0, keepdims=True)

    def count(pred):
        def body(kb, acc):
            ones = jnp.where(pred(keys_ref[rows(kb), :], kb * tk + blk_iota), 1.0, 0.0)
            parts = [ones[r * 8:(r + 1) * 8, :] for r in range(tk // 8)]
            while len(parts) > 1:
                parts = [parts[i] + parts[i + 1] for i in range(0, len(parts), 2)]
            return acc + parts[0]
        acc = lax.fori_loop(0, nkb, body, jnp.zeros((8, tq), F32))
        return jnp.sum(acc, axis=0, keepdims=True)

    @pl.when(nkb * tk > k_sel)
    def _():
        kf = float(k_sel)
        zero = jnp.zeros((1, tq), I32)
        cnt = count_top(zero)
        top = jnp.where(cnt >= kf, zero, jnp.full((1, tq), -2 ** 15, I32))

        def top_body(i, top):
            cand = top + jnp.left_shift(jnp.int32(1), 14 - i)
            return jnp.where(count_top(cand) >= kf, cand, top)

        top = lax.fori_loop(0, 15, top_body, top)

        def bit_body(i, thr):
            cand = thr + jnp.left_shift(jnp.int32(1), 15 - i)
            cnt = count(lambda blk, _: blk >= cand)
            return jnp.where(cnt >= kf, cand, thr)

        thr = lax.fori_loop(0, 16, bit_body, jnp.left_shift(top, 16))
        thr_ref[...] = jnp.broadcast_to(thr, thr_ref.shape)
        n_gt = count(lambda blk, _: blk > thr)
        n_ge = count(lambda blk, _: blk >= thr)
        need = kf - n_gt
        split = jnp.max(jnp.where(n_ge - n_gt > need, 1.0, 0.0)) > 0.0

        @pl.when(split)
        def _():
            def jbit_body(i, j0):
                cand = j0 + jnp.left_shift(jnp.int32(1), idx_bits - 1 - i)
                f = count(lambda blk, idx: (blk == thr) & (idx < cand))
                return jnp.where(f < need, cand, j0)

            j0 = lax.fori_loop(0, idx_bits, jbit_body, jnp.zeros((1, tq), I32))
            jsel_ref[...] = jnp.broadcast_to(j0 + 1, jsel_ref.shape)

    thr = thr_ref[0:1, :]
    jsel = jsel_ref[0:1, :]

    def mask_body(kb, carry):
        blk = keys_ref[rows(kb), :]
        idx = kb * tk + blk_iota
        sel = ((blk > thr) | ((blk == thr) & (idx < jsel))) & (idx < limit)
        madd_ref[rows(kb), :] = jnp.where(sel, 0.0, NEG_INF)
        return carry

    lax.fori_loop(0, nkb, mask_body, 0)

    scale = B_HEAD_DIM ** -0.5
    m_ref[...] = jnp.full(m_ref.shape, NEG_INF, F32)
    l_ref[...] = jnp.zeros(l_ref.shape, F32)
    acc_ref[...] = jnp.zeros(acc_ref.shape, F32)

    def attn_body(kb, carry, near):
        for h in range(B_HEADS):
            sl = slice(h * B_HEAD_DIM, (h + 1) * B_HEAD_DIM)
            lt = lax.dot_general(k_ref[rows(kb), sl], q_ref[:, sl], NT_DIMS, preferred_element_type=F32)
            lt = lt * (scale * LOG2E) + madd_ref[rows(kb), :]
            if near:
                lt = lt + bias_ref[h, nkb - 1 - kb, :, :tq]
            lg_ref[h] = lt
            bm_ref[h:h + 1, :] = jnp.max(lt, axis=0, keepdims=True)
        for h in range(B_HEADS):
            sl = slice(h * B_HEAD_DIM, (h + 1) * B_HEAD_DIM)
            m_old = m_ref[h:h + 1, :]
            m_new = jnp.maximum(m_old, bm_ref[h:h + 1, :])
            m_safe = jnp.where(m_new == NEG_INF, 0.0, m_new)
            alpha = jnp.exp2(m_old - m_safe)
            p = jnp.exp2(lg_ref[h] - m_safe)
            l_ref[h:h + 1, :] = alpha * l_ref[h:h + 1, :] + jnp.sum(p, axis=0, keepdims=True)
            pv = lax.dot_general(v_ref[rows(kb), sl], p.astype(BF16), TN_DIMS,
                                 preferred_element_type=F32)
            acc_ref[h] = alpha * acc_ref[h] + pv
            m_ref[h:h + 1, :] = m_new
        return carry

    n_far = jnp.maximum(nkb - 2, 0)
    lax.fori_loop(0, n_far, functools.partial(attn_body, near=False), 0)
    for h in range(B_HEADS):
        m_ref[h:h + 1, :] = m_ref[h:h + 1, :] + bias_ref[h, 2, 0:1, :tq]
    lax.fori_loop(n_far, nkb, functools.partial(attn_body, near=True), 0)
    for h in range(B_HEADS):
        sl = slice(h * B_HEAD_DIM, (h + 1) * B_HEAD_DIM)
        o_ref[:, sl] = (acc_ref[h] / l_ref[h:h + 1, :]).T


def _dsa(q, qi, small, k, v, ki, bias, pos0, s_valid, k_sel, n_query):
    b, tq_all, n_wide = q.shape
    s_pad = k.shape[1]
    tq = LANES if n_query <= LANES else TILE
    n_tiles = -(-n_query // tq)
    assert s_pad % TILE == 0 and pos0 % TILE == 0 and (tq == TILE or n_tiles == 1)
    assert n_tiles * tq <= tq_all and pos0 + n_tiles * tq <= s_pad and n_wide % B_WIDTH == 0
    q_block = n_wide // B_WIDTH - 1
    qspec = lambda w: pl.BlockSpec((None, tq, w), lambda bi, ji: (bi, ji, 0))
    kspec = lambda w: pl.BlockSpec((None, s_pad, w), lambda bi, ji: (bi, 0, 0))
    return pl.pallas_call(
        functools.partial(_dsa_kernel, pos0=pos0, s_valid=s_valid, k_sel=k_sel),
        grid=(b, n_tiles),
        in_specs=[pl.BlockSpec((None, tq, B_WIDTH), lambda bi, ji: (bi, ji, q_block)),
                  qspec(IDX_HEADS * IDX_DIM), qspec(LANES),
                  kspec(B_WIDTH), kspec(B_WIDTH), kspec(IDX_DIM),
                  pl.BlockSpec((B_HEADS, 3, TILE, TILE), lambda bi, ji: (0, 0, 0, 0))],
        out_specs=qspec(B_WIDTH),
        out_shape=jax.ShapeDtypeStruct((b, n_tiles * tq, B_WIDTH), F32),
        scratch_shapes=[pltpu.VMEM((s_pad, tq), I32),
                        pltpu.VMEM((s_pad, tq), BF16),
                        pltpu.VMEM((s_pad, tq), F32),
                        pltpu.VMEM((B_HEADS, TILE, tq), F32),
                        pltpu.VMEM((B_HEADS, B_HEAD_DIM, tq), F32),
                        pltpu.VMEM((B_HEADS, tq), F32),
                        pltpu.VMEM((B_HEADS, tq), F32),
                        pltpu.VMEM((B_HEADS, tq), F32),
                        pltpu.VMEM((8, tq), I32),
                        pltpu.VMEM((8, tq), I32)],
        compiler_params=_cparams("parallel", "arbitrary"),
        name="dsa",
    )(q, qi, small, k, v, ki, bias)


def _pack_kernel(cache_ref, new_ref, o_ref, *, n_cache_blocks):
    j = pl.program_id(1)

    @pl.when(j < n_cache_blocks)
    def _():
        o_ref[...] = cache_ref[...].reshape(o_ref.shape).astype(o_ref.dtype)

    @pl.when(j >= n_cache_blocks)
    def _():
        o_ref[...] = new_ref[...]


def _pack_rows(cache, new16):
    b, p, hh, dd = cache.shape
    tn = new16.shape[1]
    assert p % TILE == 0 and tn % TILE == 0 and new16.shape[2] == hh * dd
    ncb = p // TILE
    return pl.pallas_call(
        functools.partial(_pack_kernel, n_cache_blocks=ncb),
        grid=(b, (p + tn) // TILE),
        in_specs=[pl.BlockSpec((None, TILE, hh, dd), lambda bi, ji: (bi, jnp.minimum(ji, ncb - 1), 0, 0)),
                  pl.BlockSpec((None, TILE, hh * dd), lambda bi, ji: (bi, jnp.maximum(ji - ncb, 0), 0))],
        out_specs=pl.BlockSpec((None, TILE, hh * dd), lambda bi, ji: (bi, ji, 0)),
        out_shape=jax.ShapeDtypeStruct((b, p + tn, hh * dd), BF16),
        compiler_params=_cparams("parallel", "arbitrary"),
        name="pack_rows",
    )(cache, new16)


def _merge_kernel(x_ref, hn_ref, ob_ref, wg_ref, woa_ref, wob_ref, wo_ref, vec_ref, y_ref):
    x = x_ref[...]
    xb = x.astype(BF16)
    gate = lambda i: jnp.dot(xb, wg_ref[i], preferred_element_type=F32)
    hn = hn_ref[...] * vec_ref[0:1, :]
    branch_a = (hn * jax.nn.sigmoid(gate(0))) * jax.nn.silu(gate(1))
    branch_b = ob_ref[...] * jax.nn.silu(gate(2))
    mixed = (jax.nn.sigmoid(gate(3)) * jnp.dot(branch_a.astype(BF16), woa_ref[...], preferred_element_type=F32)
             + jax.nn.sigmoid(gate(4)) * jnp.dot(branch_b.astype(BF16), wob_ref[...], preferred_element_type=F32))
    y = DN_ALPHA * x + jnp.dot(mixed.astype(BF16), wo_ref[...], preferred_element_type=F32)
    mu = jnp.mean(y, axis=1, keepdims=True)
    var = jnp.mean(jnp.square(y - mu), axis=1, keepdims=True)
    y_ref[...] = (y - mu) * lax.rsqrt(var + LN_EPS) * vec_ref[1:2, :] + vec_ref[2:3, :]


def _merge(x2d, hn2d, ob2d, wg, woa, wob, wo, vec):
    m, d = x2d.shape
    tm = min(MERGE_TM, m)
    assert m % tm == 0
    row = pl.BlockSpec((tm, d), lambda i: (i, 0))
    const = lambda shape: pl.BlockSpec(shape, lambda i: (0,) * len(shape))
    return pl.pallas_call(
        _merge_kernel,
        grid=(m // tm,),
        in_specs=[row, row, row, const(wg.shape), const(woa.shape), const(wob.shape),
                  const(wo.shape), const(vec.shape)],
        out_specs=row,
        out_shape=jax.ShapeDtypeStruct((m, d), F32),
        compiler_params=_cparams("parallel"),
        name="merge",
    )(x2d, hn2d, ob2d, wg, woa, wob, wo, vec)


def _split_weights(w_in, b_gates):
    cuts = [0]
    for c in COL_SIZES:
        cuts.append(cuts[-1] + c)
    cols = [w_in[:, cuts[i]:cuts[i + 1]] for i in range(len(COL_SIZES))]
    (qa, ka, va, oa, za, ia, fa, qb, kb, vb, zb, qi, ki, wi, ga, gb) = cols
    pad = lambda w, n: jnp.pad(w, ((0, 0), (0, n - w.shape[1])))
    small = jnp.concatenate([pad(ia, SMALL_FG - SMALL_IG), pad(fa, SMALL_WI - SMALL_FG),
                             pad(wi, LANES - SMALL_WI)], axis=1)
    bg = jnp.concatenate([jnp.pad(b_gates[:A_HEADS], (0, SMALL_FG - SMALL_IG - A_HEADS)),
                          jnp.pad(b_gates[A_HEADS:], (0, LANES - SMALL_FG - A_HEADS))])[None, :]
    c16 = lambda w: w.astype(BF16)
    return dict(wide=c16(jnp.concatenate([qa, ka, va, qb], axis=1)), kb=c16(kb), vb=c16(vb),
                narrow=c16(jnp.concatenate([qi, small, ki], axis=1)), bg=bg.astype(F32),
                gates=jnp.stack([c16(oa), c16(za), c16(zb), c16(ga), c16(gb)]))


def _pad_rows(a, n):
    return jnp.pad(a, ((0, 0), (0, n - a.shape[1]), (0, 0)))


def _group(x, w, consts, state, cache):
    b, t, d = x.shape
    x2d = x.reshape(b * t, d)
    n_wide = w["wide"].shape[1]
    n_qi = IDX_HEADS * IDX_DIM
    (wide,) = _project(x2d, w["wide"], [(0, n_wide, BF16)], "proj_wide")
    bh = (B_HEADS, B_HEAD_DIM)
    kb32, kb16 = _project(x2d, w["kb"], [(0, B_WIDTH, F32), (0, B_WIDTH, BF16)], "proj_kb", heads=bh)
    vb32, vb16 = _project(x2d, w["vb"], [(0, B_WIDTH, F32), (0, B_WIDTH, BF16)], "proj_vb", heads=bh)
    qi, small, ki32, ki16 = _project(
        x2d, w["narrow"],
        [(0, n_qi, BF16), (n_qi, LANES, F32), (n_qi + LANES, IDX_DIM, F32), (n_qi + LANES, IDX_DIM, BF16)],
        "proj_narrow")
    r3 = lambda a: a.reshape(b, t, a.shape[-1])
    wide, kb16, vb16, qi, ki16, small = map(r3, (wide, kb16, vb16, qi, ki16, small))

    t_pad = -(-t // TILE) * TILE
    wide = _pad_rows(wide, t_pad)
    c0, n0, m0 = state
    hn, c_new, n_new, m_new = _mlstm(wide, _pad_rows(small, t_pad), w["bg"], c0, n0, m0, t_valid=t)

    if cache is None:
        pos0, keys_k, keys_v, keys_i = 0, kb16, vb16, ki16
    else:
        cache_k, cache_v, cache_i = cache
        pos0 = cache_k.shape[1]
        keys_k = _pack_rows(cache_k, _pad_rows(kb16, t_pad))
        keys_v = _pack_rows(cache_v, _pad_rows(vb16, t_pad))
        keys_i = jnp.concatenate([cache_i.astype(BF16), ki16], axis=1)
    s_valid = pos0 + t
    s_pad = pos0 + t_pad
    k_sel = min(TOPK_MAX, s_valid // 4)
    ob = _dsa(wide, _pad_rows(qi, t_pad), _pad_rows(small, t_pad),
              _pad_rows(keys_k, s_pad), _pad_rows(keys_v, s_pad), _pad_rows(keys_i, s_pad),
              consts["bias"], pos0=pos0, s_valid=s_valid, k_sel=k_sel, n_query=t)

    y = _merge(x2d, hn[:, :t].reshape(b * t, A_WIDTH), ob[:, :t].reshape(b * t, B_WIDTH),
               w["gates"], consts["woa"], consts["wob"], consts["wo"], consts["vec"])
    return (y.reshape(b, t, d), kb32.reshape(b, t, B_HEADS, B_HEAD_DIM),
            vb32.reshape(b, t, B_HEADS, B_HEAD_DIM), ki32.reshape(b, t, IDX_DIM),
            c_new, n_new, m_new)


def kernel(x_prompt, x_sample, cache_k, cache_v, cache_idx_k, state_C, state_n, state_m,
           w_in, b_gates, a_norm_g, w_out_a, w_out_b, w_o, rel_bias, ln_g, ln_b):
    w = _split_weights(w_in, b_gates)
    consts = dict(bias=_bias_tiles(rel_bias), woa=w_out_a.astype(BF16), wob=w_out_b.astype(BF16),
                  wo=w_o.astype(BF16),
                  vec=jnp.pad(jnp.stack([a_norm_g, ln_g, ln_b]).astype(F32), ((0, 5), (0, 0))))
    bp = x_prompt.shape[0]
    zero_state = (jnp.zeros((bp, A_HEADS, A_HEAD_DIM, A_HEAD_DIM), F32),
                  jnp.zeros((bp, A_HEADS, A_HEAD_DIM), F32), jnp.zeros((bp, A_HEADS), F32))
    y_p, k_p, v_p, i_p, c_p, n_p, m_p = _group(x_prompt, w, consts, zero_state, None)
    y_s, k_s, v_s, i_s, c_s, n_s, m_s = _group(
        x_sample, w, consts,
        (state_C.astype(F32), state_n.astype(F32), state_m.astype(F32)),
        (cache_k, cache_v, cache_idx_k))
    return (y_p, y_s, k_p, v_p, i_p, c_p, n_p, m_p, k_s, v_s, i_s, c_s, n_s, m_s)
```

```python
import functools
import math

import jax
import jax.numpy as jnp
from jax import lax
from jax.experimental import pallas as pl
from jax.experimental.pallas import tpu as pltpu

F32 = jnp.float32
BF16 = jnp.bfloat16
I32 = jnp.int32

D_MODEL = 1024
DEPTH = 1
CHUNK = 64
A_HEADS = 4
A_HEAD_DIM = 256
A_WIDTH = A_HEADS * A_HEAD_DIM
B_HEADS = 8
B_HEAD_DIM = 128
B_WIDTH = B_HEADS * B_HEAD_DIM
IDX_HEADS = 8
IDX_DIM = 64
TOPK_MAX = 256
N_BUCKETS = 32
MAX_DISTANCE = 128
LN_EPS = 1e-5
HEAD_NORM_EPS = 1e-6
DN_ALPHA = (2 * DEPTH) ** 0.25

COL_SIZES = (A_WIDTH, A_WIDTH, A_WIDTH, A_WIDTH, A_WIDTH, A_HEADS, A_HEADS,
             B_WIDTH, B_WIDTH, B_WIDTH, B_WIDTH, IDX_HEADS * IDX_DIM, IDX_DIM, IDX_HEADS,
             D_MODEL, D_MODEL)

LANES = 128
MXU_DIM = 256
VMEM_LIMIT_BYTES = 56 * 1024 * 1024

TILE = MXU_DIM
PROJ_TM = 1024
PROJ_TN = 1024
MERGE_TM = 256

SMALL_IG = 0
SMALL_FG = 8
SMALL_WI = 16

LOG2E = math.log2(math.e)
A_K_SCALE = A_HEAD_DIM ** -0.5
assert math.frexp(A_K_SCALE)[0] == 0.5
B_Q_SCALE = B_HEAD_DIM ** -0.5 * LOG2E
INT_MIN = -2 ** 31
F32_MIN_NORMAL = 2.0 ** -126
KEY_NEG_INF = (0xFF800000 ^ 0x7FFFFFFF) - 2 ** 32
KEY_POS_INF = 0x7F800000
KEY_MIN_NORMAL = 0x00800000
NEG_INF = float("-inf")
NT_DIMS = (((1,), (1,)), ((), ()))
TN_DIMS = (((0,), (0,)), ((), ()))


def _cparams(*sem):
    return pltpu.CompilerParams(dimension_semantics=sem, vmem_limit_bytes=VMEM_LIMIT_BYTES)


N_WIDE_TILES = 4
N_PROJ_TILES = 7
N_QI = IDX_HEADS * IDX_DIM
assert A_WIDTH == B_WIDTH == PROJ_TN and N_QI + LANES + IDX_DIM <= PROJ_TN


def _proj_kernel(x_ref, w_ref, wide_ref, kb32_ref, kb16_ref, vb32_ref, vb16_ref,
                 qi_ref, small_ref, ki32_ref, ki16_ref):
    j = pl.program_id(1)
    acc = jnp.dot(x_ref[...].astype(BF16), w_ref[...], preferred_element_type=F32)

    @pl.when(j < N_WIDE_TILES - 1)
    def _():
        wide_ref[...] = acc.astype(BF16)

    @pl.when(j == N_WIDE_TILES - 1)
    def _():
        wide_ref[...] = (acc * B_Q_SCALE).astype(BF16)

    @pl.when(j == N_WIDE_TILES)
    def _():
        kb32_ref[...] = acc.reshape(kb32_ref.shape)
        kb16_ref[...] = acc.astype(BF16)

    @pl.when(j == N_WIDE_TILES + 1)
    def _():
        vb32_ref[...] = acc.reshape(vb32_ref.shape)
        vb16_ref[...] = acc.astype(BF16)

    @pl.when(j == N_WIDE_TILES + 2)
    def _():
        qi_ref[...] = acc[:, :N_QI].astype(BF16)
        small_ref[...] = acc[:, N_QI:N_QI + LANES]
        ki = acc[:, N_QI + LANES:N_QI + LANES + IDX_DIM]
        ki32_ref[...] = ki
        ki16_ref[...] = ki.astype(BF16)


def _project(x2d, w):
    m, k = x2d.shape
    tm = min(PROJ_TM, m)
    assert m % tm == 0 and w.shape == (k, N_PROJ_TILES * PROJ_TN)
    row = lambda width: pl.BlockSpec((tm, width), lambda i, j: (i, 0))
    heads = pl.BlockSpec((tm, B_HEADS, B_HEAD_DIM), lambda i, j: (i, 0, 0))
    sds = jax.ShapeDtypeStruct
    return pl.pallas_call(
        _proj_kernel,
        grid=(m // tm, N_PROJ_TILES),
        in_specs=[pl.BlockSpec((tm, k), lambda i, j: (i, 0)),
                  pl.BlockSpec((k, PROJ_TN), lambda i, j: (0, j))],
        out_specs=[pl.BlockSpec((tm, PROJ_TN), lambda i, j: (i, jnp.minimum(j, N_WIDE_TILES - 1))),
                   heads, row(B_WIDTH), heads, row(B_WIDTH),
                   row(N_QI), row(LANES), row(IDX_DIM), row(IDX_DIM)],
        out_shape=[sds((m, N_WIDE_TILES * PROJ_TN), BF16),
                   sds((m, B_HEADS, B_HEAD_DIM), F32), sds((m, B_WIDTH), BF16),
                   sds((m, B_HEADS, B_HEAD_DIM), F32), sds((m, B_WIDTH), BF16),
                   sds((m, N_QI), BF16), sds((m, LANES), F32), sds((m, IDX_DIM), F32), sds((m, IDX_DIM), BF16)],
        compiler_params=_cparams("parallel", "arbitrary"),
        name="proj",
    )(x2d, w)


def _gates_kernel(sm_ref, bg_ref, grow_ref, gcol_ref, *, t_valid):
    z = sm_ref[...] + bg_ref[...]
    col = lax.broadcasted_iota(I32, z.shape, 1)
    tok = lax.broadcasted_iota(I32, z.shape, 0)
    is_ig = col < SMALL_FG
    log_sig = jnp.minimum(z, 0.0) - jnp.log1p(jnp.exp(-jnp.abs(z)))
    gl = jnp.where(is_ig, z, log_sig)
    gl = jnp.where(tok < t_valid, gl, jnp.where(is_ig, NEG_INF, 0.0))
    gl_t = gl.T
    ig_t = gl_t[SMALL_IG:SMALL_IG + 8, :]
    fg_t = gl_t[SMALL_FG:SMALL_FG + 8, :]
    in_chunk = lax.broadcasted_iota(I32, fg_t.shape, 1) % TILE
    shift = 1
    while shift < TILE:
        fg_t = fg_t + jnp.where(in_chunk >= shift, pltpu.roll(fg_t, shift, 1), 0.0)
        shift *= 2
    rows = jnp.concatenate([ig_t - fg_t, fg_t], axis=0)
    grow_ref[...] = rows
    pad = jnp.zeros((LANES - rows.shape[0], rows.shape[1]), F32)
    gcol_ref[...] = jnp.concatenate([rows, pad], axis=0).T


def _gates(small, bg, t_valid):
    b, t, _ = small.shape
    return pl.pallas_call(
        functools.partial(_gates_kernel, t_valid=t_valid),
        grid=(b,),
        in_specs=[pl.BlockSpec((None, t, LANES), lambda bi: (bi, 0, 0)),
                  pl.BlockSpec((1, LANES), lambda bi: (0, 0))],
        out_specs=[pl.BlockSpec((None, 16, t), lambda bi: (bi, 0, 0)),
                   pl.BlockSpec((None, t, LANES), lambda bi: (bi, 0, 0))],
        out_shape=[jax.ShapeDtypeStruct((b, 16, t), F32), jax.ShapeDtypeStruct((b, t, LANES), F32)],
        compiler_params=_cparams("parallel"),
        name="gates",
    )(small, bg)


def _mlstm_kernel(q_ref, k_ref, v_ref, grow_ref, gcol_ref, c0_ref, n0_ref, m0_ref,
                  hn_ref, c_ref, n_ref, m_ref, ct_ref):
    L = q_ref.shape[0]
    c = pl.program_id(1)

    @pl.when(c == 0)
    def _():
        for h in range(A_HEADS):
            ct_ref[h] = c0_ref[h].T
        n_ref[...] = n0_ref[...]
        m_ref[...] = m0_ref[...]

    causal_t = lax.broadcasted_iota(I32, (L, L), 0) <= lax.broadcasted_iota(I32, (L, L), 1)

    for h in range(A_HEADS):
        sl = slice(h * A_HEAD_DIM, (h + 1) * A_HEAD_DIM)
        b_row = grow_ref[SMALL_FG + h:SMALL_FG + h + 1, :]
        c_col = gcol_ref[:, SMALL_IG + h:SMALL_IG + h + 1]
        m_old = m_ref[h][0:1, 0:1]
        q = q_ref[:, sl]
        k = k_ref[:, sl]
        v = v_ref[:, sl]
        ct_old = ct_ref[h]
        n_old = n_ref[h]

        dmat_t = jnp.where(causal_t, b_row + c_col, NEG_INF)
        a_inter = b_row + m_old
        m_j = jnp.maximum(a_inter, jnp.max(dmat_t, axis=0, keepdims=True))
        inter = jnp.exp(a_inter - m_j)
        s_t = (lax.dot_general(k, q, NT_DIMS, preferred_element_type=F32) * A_K_SCALE
               * jnp.exp(dmat_t - m_j))
        num_t = (lax.dot_general(v, s_t.astype(BF16), TN_DIMS, preferred_element_type=F32)
                 + lax.dot_general(ct_old.astype(BF16), q, NT_DIMS, preferred_element_type=F32) * inter)
        n_rows = jnp.broadcast_to(n_old[0:1, :], (16, A_HEAD_DIM)).astype(BF16)
        qn = lax.dot_general(n_rows, q, NT_DIMS, preferred_element_type=F32)[0:1, :]
        den = jnp.sum(s_t, axis=0, keepdims=True) + inter * qn
        hh_t = num_t / jnp.maximum(jnp.abs(den), jnp.exp(-m_j))
        mu = jnp.mean(hh_t, axis=0, keepdims=True)
        var = jnp.mean(jnp.square(hh_t - mu), axis=0, keepdims=True)
        hn_ref[:, sl] = ((hh_t - mu) * lax.rsqrt(var + HEAD_NORM_EPS)).T

        b_last = b_row[:, L - 1:L]
        g_col = b_last + c_col
        m_new = jnp.maximum(b_last + m_old, jnp.max(g_col, axis=0, keepdims=True))
        decay = jnp.exp(b_last + m_old - m_new)
        w_col = jnp.exp(g_col - m_new) * A_K_SCALE
        kw = k.astype(F32) * w_col
        ct_ref[h] = decay * ct_old + lax.dot_general(v, kw.astype(BF16), TN_DIMS,
                                                     preferred_element_type=F32)
        n_ref[h] = decay * n_old + jnp.sum(kw, axis=0, keepdims=True)
        m_ref[h] = jnp.broadcast_to(m_new, (8, LANES))

    @pl.when(c == pl.num_programs(1) - 1)
    def _():
        for h in range(A_HEADS):
            c_ref[h] = ct_ref[h].T


def _mlstm(qkv, small, bg, c0, n0, m0, t_valid):
    b, t, _ = qkv.shape
    assert t % TILE == 0
    grow, gcol = _gates(small, bg, t_valid)
    n0x = jnp.broadcast_to(n0[:, :, None, :], (b, A_HEADS, 8, A_HEAD_DIM))
    m0x = jnp.broadcast_to(m0[:, :, None, None], (b, A_HEADS, 8, LANES))
    qkv_spec = lambda j: pl.BlockSpec((None, TILE, A_WIDTH), lambda bi, ci: (bi, ci, j))
    st_c = pl.BlockSpec((None, A_HEADS, A_HEAD_DIM, A_HEAD_DIM), lambda bi, ci: (bi, 0, 0, 0))
    st_n = pl.BlockSpec((None, A_HEADS, 8, A_HEAD_DIM), lambda bi, ci: (bi, 0, 0, 0))
    st_m = pl.BlockSpec((None, A_HEADS, 8, LANES), lambda bi, ci: (bi, 0, 0, 0))
    hn, c_new, n_new, m_new = pl.pallas_call(
        _mlstm_kernel,
        grid=(b, t // TILE),
        in_specs=[qkv_spec(0), qkv_spec(1), qkv_spec(2),
                  pl.BlockSpec((None, 16, TILE), lambda bi, ci: (bi, 0, ci)),
                  pl.BlockSpec((None, TILE, LANES), lambda bi, ci: (bi, ci, 0)),
                  st_c, st_n, st_m],
        out_specs=[pl.BlockSpec((None, TILE, A_WIDTH), lambda bi, ci: (bi, ci, 0)),
                   st_c, st_n, st_m],
        out_shape=[jax.ShapeDtypeStruct((b, t, A_WIDTH), F32),
                   jax.ShapeDtypeStruct((b, A_HEADS, A_HEAD_DIM, A_HEAD_DIM), F32),
                   jax.ShapeDtypeStruct((b, A_HEADS, 8, A_HEAD_DIM), F32),
                   jax.ShapeDtypeStruct((b, A_HEADS, 8, LANES), F32)],
        scratch_shapes=[pltpu.VMEM((A_HEADS, A_HEAD_DIM, A_HEAD_DIM), F32)],
        compiler_params=_cparams("parallel", "arbitrary"),
        name="mlstm",
    )(qkv, qkv, qkv, grow, gcol, c0, n0x, m0x)
    return hn, c_new, n_new[:, :, 0, :], m_new[:, :, 0, 0]


def _t5_bucket(rel):
    half = N_BUCKETS // 2
    max_exact = half // 2
    ret = jnp.where(rel > 0, half, 0)
    n = jnp.abs(rel)
    nf = jnp.maximum(n, 1).astype(F32)
    large = max_exact + (jnp.log(nf / max_exact) / math.log(MAX_DISTANCE / max_exact)
                         * (half - max_exact)).astype(I32)
    large = jnp.minimum(large, half - 1)
    return ret + jnp.where(n < max_exact, n, large)


def _bias_kernel(rb_ref, out_ref):
    kk = lax.broadcasted_iota(I32, (TILE, TILE), 0)
    qq = lax.broadcasted_iota(I32, (TILE, TILE), 1)
    for ti, off in enumerate((0, -TILE, -2 * TILE)):
        bucket = _t5_bucket(kk + off - qq)
        for h in range(B_HEADS):
            acc = jnp.zeros((TILE, TILE), F32)
            for bkt in range(N_BUCKETS):
                acc = jnp.where(bucket == bkt, rb_ref[bkt, h], acc)
            out_ref[h, ti] = acc * LOG2E


def _bias_tiles(rel_bias):
    return pl.pallas_call(
        _bias_kernel,
        in_specs=[pl.BlockSpec(memory_space=pltpu.SMEM)],
        out_specs=pl.BlockSpec(memory_space=pltpu.VMEM),
        out_shape=jax.ShapeDtypeStruct((B_HEADS, 3, TILE, TILE), F32),
        compiler_params=pltpu.CompilerParams(vmem_limit_bytes=VMEM_LIMIT_BYTES),
        name="bias_tiles",
    )(rel_bias.astype(F32))


def _dsa_kernel(q_ref, qi_ref, sm_ref, k_ref, v_ref, ki_ref, bias_ref, o_ref,
                keys_ref, top_ref, madd_ref, lg_ref, acc_ref, bm_ref, m_ref, l_ref, thr_ref, jsel_ref,
                *, pos0, s_valid, k_sel):
    tq = q_ref.shape[0]
    tk = TILE
    t0 = pos0 + pl.program_id(1) * tq
    nkb = t0 // tk + 1
    qpos = t0 + lax.broadcasted_iota(I32, (1, tq), 1)
    limit = jnp.minimum((qpos // CHUNK + 1) * CHUNK, s_valid)
    blk_iota = lax.broadcasted_iota(I32, (tk, tq), 0)
    idx_bits = int(k_ref.shape[0]).bit_length()

    def rows(kb):
        return pl.ds(pl.multiple_of(kb * tk, tk), tk)

    w_t = sm_ref[...].T
    w_heads = [w_t[SMALL_WI + h:SMALL_WI + h + 1, :] * (IDX_HEADS ** -0.5) * (IDX_DIM ** -0.5)
               for h in range(IDX_HEADS)]

    def score_body(kb, carry):
        ki = ki_ref[rows(kb), :]
        sc = jnp.zeros((tk, tq), F32)
        for h in range(IDX_HEADS):
            qh = qi_ref[:, h * IDX_DIM:(h + 1) * IDX_DIM]
            d = lax.dot_general(ki, qh, NT_DIMS, preferred_element_type=F32)
            sc = sc + w_heads[h] * jnp.maximum(d, 0.0)
        sc = jnp.where(jnp.abs(sc) < F32_MIN_NORMAL, 0.0, sc)
        bits = lax.bitcast_convert_type(sc, I32)
        key = bits ^ ((bits >> 31) & 0x7FFFFFFF)
        adm = (kb * tk + blk_iota) < limit
        keys_ref[rows(kb), :] = jnp.where(adm, key, KEY_NEG_INF)
        top = lax.bitcast_convert_type(bits & jnp.int32(-65536), F32)
        top_ref[rows(kb), :] = jnp.where(adm, top, NEG_INF).astype(BF16)
        return carry

    lax.fori_loop(0, nkb, score_body, 0)

    thr_ref[...] = jnp.full(thr_ref.shape, INT_MIN, I32)
    jsel_ref[...] = jnp.full(jsel_ref.shape, 2 ** 30, I32)

    def count_top(cand16):
        bits16 = cand16 ^ ((cand16 >> 15) & 0x7FFF)
        cand = lax.bitcast_convert_type(jnp.left_shift(bits16, 16), F32)
        cand = jnp.where((cand16 > 0) & (cand16 < KEY_MIN_NORMAL >> 16), F32_MIN_NORMAL, cand)
        cand = jnp.where(cand16 < KEY_NEG_INF >> 16, NEG_INF, cand)
        cand = jnp.where(cand16 > KEY_POS_INF >> 16, -NEG_INF, cand).astype(BF16)
        one, zero = jnp.ones((), BF16), jnp.zeros((), BF16)

        def body(kb, acc):
            ones = jnp.where(top_ref[rows(kb), :] >= cand, one, zero)
            parts = [ones[r * 16:(r + 1) * 16, :] for r in range(tk // 16)]
            while len(parts) > 1:
                parts = [parts[i] + parts[i + 1] for i in range(0, len(parts), 2)]
            return acc + parts[0].astype(F32)
        acc = lax.fori_loop(0, nkb, body, jnp.zeros((16, tq), F32))
        return jnp.sum(acc, axis=0, keepdims=True)

    def count(pred):
        def body(kb, acc):
            ones = jnp.where(pred(keys_ref[rows(kb), :], kb * tk + blk_iota), 1.0, 0.0)
            parts = [ones[r * 8:(r + 1) * 8, :] for r in range(tk // 8)]
            while len(parts) > 1:
                parts = [parts[i] + parts[i + 1] for i in range(0, len(parts), 2)]
            return acc + parts[0]
        acc = lax.fori_loop(0, nkb, body, jnp.zeros((8, tq), F32))
        return jnp.sum(acc, axis=0, keepdims=True)

    @pl.when(nkb * tk > k_sel)
    def _():
        kf = float(k_sel)
        zero = jnp.zeros((1, tq), I32)
        cnt = count_top(zero)
        top = jnp.where(cnt >= kf, zero, jnp.full((1, tq), -2 ** 15, I32))

        def top_body(i, top):
            cand = top + jnp.left_shift(jnp.int32(1), 14 - i)
            return jnp.where(count_top(cand) >= kf, cand, top)

        top = lax.fori_loop(0, 15, top_body, top)

        def bit_body(i, thr):
            cand = thr + jnp.left_shift(jnp.int32(1), 15 - i)
            cnt = count(lambda blk, _: blk >= cand)
            return jnp.where(cnt >= kf, cand, thr)

        thr = lax.fori_loop(0, 16, bit_body, jnp.left_shift(top, 16))
        thr_ref[...] = jnp.broadcast_to(thr, thr_ref.shape)
        n_gt = count(lambda blk, _: blk > thr)
        n_ge = count(lambda blk, _: blk >= thr)
        need = kf - n_gt
        split = jnp.max(jnp.where(n_ge - n_gt > need, 1.0, 0.0)) > 0.0

        @pl.when(split)
        def _():
            def jbit_body(i, j0):
                cand = j0 + jnp.left_shift(jnp.int32(1), idx_bits - 1 - i)
                f = count(lambda blk, idx: (blk == thr) & (idx < cand))
                return jnp.where(f < need, cand, j0)

            j0 = lax.fori_loop(0, idx_bits, jbit_body, jnp.zeros((1, tq), I32))
            jsel_ref[...] = jnp.broadcast_to(j0 + 1, jsel_ref.shape)

    thr = thr_ref[0:1, :]
    jsel = jsel_ref[0:1, :]

    def mask_body(kb, carry):
        blk = keys_ref[rows(kb), :]
        idx = kb * tk + blk_iota
        sel = ((blk > thr) | ((blk == thr) & (idx < jsel))) & (idx < limit)
        madd_ref[rows(kb), :] = jnp.where(sel, 0.0, NEG_INF)
        return carry

    lax.fori_loop(0, nkb, mask_body, 0)

    m_ref[...] = jnp.full(m_ref.shape, NEG_INF, F32)
    l_ref[...] = jnp.zeros(l_ref.shape, F32)
    acc_ref[...] = jnp.zeros(acc_ref.shape, F32)

    def attn_body(kb, carry, near):
        for h in range(B_HEADS):
            sl = slice(h * B_HEAD_DIM, (h + 1) * B_HEAD_DIM)
            lt = lax.dot_general(k_ref[rows(kb), sl], q_ref[:, sl], NT_DIMS, preferred_element_type=F32)
            lt = lt + madd_ref[rows(kb), :]
            if near:
                lt = lt + bias_ref[h, nkb - 1 - kb, :, :tq]
            lg_ref[h] = lt
            bm_ref[h:h + 1, :] = jnp.max(lt, axis=0, keepdims=True)
        for h in range(B_HEADS):
            sl = slice(h * B_HEAD_DIM, (h + 1) * B_HEAD_DIM)
            m_old = m_ref[h:h + 1, :]
            m_new = jnp.maximum(m_old, bm_ref[h:h + 1, :])
            m_safe = jnp.where(m_new == NEG_INF, 0.0, m_new)
            alpha = jnp.exp2(m_old - m_safe)
            p = jnp.exp2(lg_ref[h] - m_safe)
            l_ref[h:h + 1, :] = alpha * l_ref[h:h + 1, :] + jnp.sum(p, axis=0, keepdims=True)
            pv = lax.dot_general(v_ref[rows(kb), sl], p.astype(BF16), TN_DIMS,
                                 preferred_element_type=F32)
            acc_ref[h] = alpha * acc_ref[h] + pv
            m_ref[h:h + 1, :] = m_new
        return carry

    n_far = jnp.maximum(nkb - 2, 0)
    lax.fori_loop(0, n_far, functools.partial(attn_body, near=False), 0)
    for h in range(B_HEADS):
        m_ref[h:h + 1, :] = m_ref[h:h + 1, :] + bias_ref[h, 2, 0:1, :tq]
    lax.fori_loop(n_far, nkb, functools.partial(attn_body, near=True), 0)
    for h in range(B_HEADS):
        sl = slice(h * B_HEAD_DIM, (h + 1) * B_HEAD_DIM)
        o_ref[:, sl] = (acc_ref[h] / l_ref[h:h + 1, :]).T


def _dsa(q, qi, small, k, v, ki, bias, pos0, s_valid, k_sel, n_query):
    b, tq_all, n_wide = q.shape
    s_pad = k.shape[1]
    tq = LANES if n_query <= LANES else TILE
    n_tiles = -(-n_query // tq)
    assert s_pad % TILE == 0 and pos0 % TILE == 0 and (tq == TILE or n_tiles == 1)
    assert n_tiles * tq <= tq_all and pos0 + n_tiles * tq <= s_pad and n_wide % B_WIDTH == 0
    q_block = n_wide // B_WIDTH - 1
    qspec = lambda w: pl.BlockSpec((None, tq, w), lambda bi, ji: (bi, ji, 0))
    kspec = lambda w: pl.BlockSpec((None, s_pad, w), lambda bi, ji: (bi, 0, 0))
    return pl.pallas_call(
        functools.partial(_dsa_kernel, pos0=pos0, s_valid=s_valid, k_sel=k_sel),
        grid=(b, n_tiles),
        in_specs=[pl.BlockSpec((None, tq, B_WIDTH), lambda bi, ji: (bi, ji, q_block)),
                  qspec(IDX_HEADS * IDX_DIM), qspec(LANES),
                  kspec(B_WIDTH), kspec(B_WIDTH), kspec(IDX_DIM),
                  pl.BlockSpec((B_HEADS, 3, TILE, TILE), lambda bi, ji: (0, 0, 0, 0))],
        out_specs=qspec(B_WIDTH),
        out_shape=jax.ShapeDtypeStruct((b, n_tiles * tq, B_WIDTH), F32),
        scratch_shapes=[pltpu.VMEM((s_pad, tq), I32),
                        pltpu.VMEM((s_pad, tq), BF16),
                        pltpu.VMEM((s_pad, tq), F32),
                        pltpu.VMEM((B_HEADS, TILE, tq), F32),
                        pltpu.VMEM((B_HEADS, B_HEAD_DIM, tq), F32),
                        pltpu.VMEM((B_HEADS, tq), F32),
                        pltpu.VMEM((B_HEADS, tq), F32),
                        pltpu.VMEM((B_HEADS, tq), F32),
                        pltpu.VMEM((8, tq), I32),
                        pltpu.VMEM((8, tq), I32)],
        compiler_params=_cparams("parallel", "arbitrary"),
        name="dsa",
    )(q, qi, small, k, v, ki, bias)


def _pack_kernel(cache_ref, new_ref, o_ref, *, n_cache_blocks):
    j = pl.program_id(1)

    @pl.when(j < n_cache_blocks)
    def _():
        o_ref[...] = cache_ref[...].reshape(o_ref.shape).astype(o_ref.dtype)

    @pl.when(j >= n_cache_blocks)
    def _():
        o_ref[...] = new_ref[...]


def _pack_rows(cache, new16):
    b, p, hh, dd = cache.shape
    tn = new16.shape[1]
    assert p % TILE == 0 and tn % TILE == 0 and new16.shape[2] == hh * dd
    ncb = p // TILE
    return pl.pallas_call(
        functools.partial(_pack_kernel, n_cache_blocks=ncb),
        grid=(b, (p + tn) // TILE),
        in_specs=[pl.BlockSpec((None, TILE, hh, dd), lambda bi, ji: (bi, jnp.minimum(ji, ncb - 1), 0, 0)),
                  pl.BlockSpec((None, TILE, hh * dd), lambda bi, ji: (bi, jnp.maximum(ji - ncb, 0), 0))],
        out_specs=pl.BlockSpec((None, TILE, hh * dd), lambda bi, ji: (bi, ji, 0)),
        out_shape=jax.ShapeDtypeStruct((b, p + tn, hh * dd), BF16),
        compiler_params=_cparams("parallel", "arbitrary"),
        name="pack_rows",
    )(cache, new16)


def _merge_kernel(x_ref, hn_ref, ob_ref, wg_ref, woa_ref, wob_ref, wo_ref, vec_ref, y_ref):
    x = x_ref[...]
    xb = x.astype(BF16)
    gate = lambda i: jnp.dot(xb, wg_ref[i], preferred_element_type=F32)
    hn = hn_ref[...] * vec_ref[0:1, :]
    branch_a = (hn * jax.nn.sigmoid(gate(0))) * jax.nn.silu(gate(1))
    branch_b = ob_ref[...] * jax.nn.silu(gate(2))
    mixed = (jax.nn.sigmoid(gate(3)) * jnp.dot(branch_a.astype(BF16), woa_ref[...], preferred_element_type=F32)
             + jax.nn.sigmoid(gate(4)) * jnp.dot(branch_b.astype(BF16), wob_ref[...], preferred_element_type=F32))
    y = DN_ALPHA * x + jnp.dot(mixed.astype(BF16), wo_ref[...], preferred_element_type=F32)
    mu = jnp.mean(y, axis=1, keepdims=True)
    var = jnp.mean(jnp.square(y - mu), axis=1, keepdims=True)
    y_ref[...] = (y - mu) * lax.rsqrt(var + LN_EPS) * vec_ref[1:2, :] + vec_ref[2:3, :]


def _merge(x2d, hn2d, ob2d, wg, woa, wob, wo, vec):
    m, d = x2d.shape
    tm = min(MERGE_TM, m)
    assert m % tm == 0
    row = pl.BlockSpec((tm, d), lambda i: (i, 0))
    const = lambda shape: pl.BlockSpec(shape, lambda i: (0,) * len(shape))
    return pl.pallas_call(
        _merge_kernel,
        grid=(m // tm,),
        in_specs=[row, row, row, const(wg.shape), const(woa.shape), const(wob.shape),
                  const(wo.shape), const(vec.shape)],
        out_specs=row,
        out_shape=jax.ShapeDtypeStruct((m, d), F32),
        compiler_params=_cparams("parallel"),
        name="merge",
    )(x2d, hn2d, ob2d, wg, woa, wob, wo, vec)


def _split_weights(w_in, b_gates):
    cuts = [0]
    for c in COL_SIZES:
        cuts.append(cuts[-1] + c)
    cols = [w_in[:, cuts[i]:cuts[i + 1]] for i in range(len(COL_SIZES))]
    (qa, ka, va, oa, za, ia, fa, qb, kb, vb, zb, qi, ki, wi, ga, gb) = cols
    pad = lambda w, n: jnp.pad(w, ((0, 0), (0, n - w.shape[1])))
    small = jnp.concatenate([pad(ia, SMALL_FG - SMALL_IG), pad(fa, SMALL_WI - SMALL_FG),
                             pad(wi, LANES - SMALL_WI)], axis=1)
    bg = jnp.concatenate([jnp.pad(b_gates[:A_HEADS], (0, SMALL_FG - SMALL_IG - A_HEADS)),
                          jnp.pad(b_gates[A_HEADS:], (0, LANES - SMALL_FG - A_HEADS))])[None, :]
    c16 = lambda w: w.astype(BF16)
    narrow = pad(jnp.concatenate([qi, small, ki], axis=1), PROJ_TN)
    return dict(proj=c16(jnp.concatenate([qa, ka, va, qb, kb, vb, narrow], axis=1)), bg=bg.astype(F32),
                gates=jnp.stack([c16(oa), c16(za), c16(zb), c16(ga), c16(gb)]))


def _pad_rows(a, n):
    return jnp.pad(a, ((0, 0), (0, n - a.shape[1]), (0, 0)))


def _group(x, w, consts, state, cache):
    b, t, d = x.shape
    x2d = x.reshape(b * t, d)
    wide, kb32, kb16, vb32, vb16, qi, small, ki32, ki16 = _project(x2d, w["proj"])
    r3 = lambda a: a.reshape(b, t, a.shape[-1])
    wide, kb16, vb16, qi, ki16, small = map(r3, (wide, kb16, vb16, qi, ki16, small))

    t_pad = -(-t // TILE) * TILE
    wide = _pad_rows(wide, t_pad)
    c0, n0, m0 = state
    hn, c_new, n_new, m_new = _mlstm(wide, _pad_rows(small, t_pad), w["bg"], c0, n0, m0, t_valid=t)

    if cache is None:
        pos0, keys_k, keys_v, keys_i = 0, kb16, vb16, ki16
    else:
        cache_k, cache_v, cache_i = cache
        pos0 = cache_k.shape[1]
        keys_k = _pack_rows(cache_k, _pad_rows(kb16, t_pad))
        keys_v = _pack_rows(cache_v, _pad_rows(vb16, t_pad))
        keys_i = jnp.concatenate([cache_i.astype(BF16), ki16], axis=1)
    s_valid = pos0 + t
    s_pad = pos0 + t_pad
    k_sel = min(TOPK_MAX, s_valid // 4)
    ob = _dsa(wide, _pad_rows(qi, t_pad), _pad_rows(small, t_pad),
              _pad_rows(keys_k, s_pad), _pad_rows(keys_v, s_pad), _pad_rows(keys_i, s_pad),
              consts["bias"], pos0=pos0, s_valid=s_valid, k_sel=k_sel, n_query=t)

    y = _merge(x2d, hn[:, :t].reshape(b * t, A_WIDTH), ob[:, :t].reshape(b * t, B_WIDTH),
               w["gates"], consts["woa"], consts["wob"], consts["wo"], consts["vec"])
    return (y.reshape(b, t, d), kb32.reshape(b, t, B_HEADS, B_HEAD_DIM),
            vb32.reshape(b, t, B_HEADS, B_HEAD_DIM), ki32.reshape(b, t, IDX_DIM),
            c_new, n_new, m_new)


def kernel(x_prompt, x_sample, cache_k, cache_v, cache_idx_k, state_C, state_n, state_m,
           w_in, b_gates, a_norm_g, w_out_a, w_out_b, w_o, rel_bias, ln_g, ln_b):
    w = _split_weights(w_in, b_gates)
    consts = dict(bias=_bias_tiles(rel_bias), woa=w_out_a.astype(BF16), wob=w_out_b.astype(BF16),
                  wo=w_o.astype(BF16),
                  vec=jnp.pad(jnp.stack([a_norm_g, ln_g, ln_b]).astype(F32), ((0, 5), (0, 0))))
    bp = x_prompt.shape[0]
    zero_state = (jnp.zeros((bp, A_HEADS, A_HEAD_DIM, A_HEAD_DIM), F32),
                  jnp.zeros((bp, A_HEADS, A_HEAD_DIM), F32), jnp.zeros((bp, A_HEADS), F32))
    y_p, k_p, v_p, i_p, c_p, n_p, m_p = _group(x_prompt, w, consts, zero_state, None)
    y_s, k_s, v_s, i_s, c_s, n_s, m_s = _group(
        x_sample, w, consts,
        (state_C.astype(F32), state_n.astype(F32), state_m.astype(F32)),
        (cache_k, cache_v, cache_idx_k))
    return (y_p, y_s, k_p, v_p, i_p, c_p, n_p, m_p, k_s, v_s, i_s, c_s, n_s, m_s)
```

```python
import functools
import math

import jax
import jax.numpy as jnp
from jax import lax
from jax.experimental import pallas as pl
from jax.experimental.pallas import tpu as pltpu

F32 = jnp.float32
BF16 = jnp.bfloat16
I32 = jnp.int32

D_MODEL = 1024
DEPTH = 1
CHUNK = 64
A_HEADS = 4
A_HEAD_DIM = 256
A_WIDTH = A_HEADS * A_HEAD_DIM
B_HEADS = 8
B_HEAD_DIM = 128
B_WIDTH = B_HEADS * B_HEAD_DIM
IDX_HEADS = 8
IDX_DIM = 64
TOPK_MAX = 256
N_BUCKETS = 32
MAX_DISTANCE = 128
LN_EPS = 1e-5
HEAD_NORM_EPS = 1e-6
DN_ALPHA = (2 * DEPTH) ** 0.25

COL_SIZES = (A_WIDTH, A_WIDTH, A_WIDTH, A_WIDTH, A_WIDTH, A_HEADS, A_HEADS,
             B_WIDTH, B_WIDTH, B_WIDTH, B_WIDTH, IDX_HEADS * IDX_DIM, IDX_DIM, IDX_HEADS,
             D_MODEL, D_MODEL)

LANES = 128
MXU_DIM = 256
VMEM_LIMIT_BYTES = 56 * 1024 * 1024

TILE = MXU_DIM
PROJ_TM = 1024
PROJ_TN = 1024
MERGE_TM = 256
PACK_ROWS = 1024

SMALL_IG = 0
SMALL_FG = 8
SMALL_WI = 16

LOG2E = math.log2(math.e)
A_K_SCALE = A_HEAD_DIM ** -0.5
B_Q_SCALE = B_HEAD_DIM ** -0.5 * LOG2E
INT_MIN = -2 ** 31
F32_MIN_NORMAL = 2.0 ** -126
KEY_NEG_INF = (0xFF800000 ^ 0x7FFFFFFF) - 2 ** 32
KEY_POS_INF = 0x7F800000
KEY_MIN_NORMAL = 0x00800000
NEG_INF = float("-inf")
NT_DIMS = (((1,), (1,)), ((), ()))
TN_DIMS = (((0,), (0,)), ((), ()))


def _cparams(*sem):
    return pltpu.CompilerParams(dimension_semantics=sem, vmem_limit_bytes=VMEM_LIMIT_BYTES)


def _proj_kernel(x_ref, w_ref, *o_refs, col_slices, tile_scale):
    acc = jnp.dot(x_ref[...].astype(BF16), w_ref[...], preferred_element_type=F32)
    if tile_scale is not None:
        acc = acc * jnp.where(pl.program_id(1) == tile_scale[0], tile_scale[1], 1.0)
    for o_ref, (c0, cw) in zip(o_refs, col_slices):
        o_ref[...] = acc[:, c0:c0 + cw].reshape(o_ref.shape).astype(o_ref.dtype)


def _project(x2d, w, outs, name, heads=None, tile_scale=None):
    m, k = x2d.shape
    n = w.shape[1]
    tm = min(PROJ_TM, m)
    full = all((c0, cw) == (0, n) for c0, cw, _ in outs)
    tn = min(PROJ_TN, n) if full else n
    assert m % tm == 0 and n % tn == 0
    col_slices = tuple((0, tn) if full else (c0, cw) for c0, cw, _ in outs)
    out_specs, out_shape = [], []
    for (c0, cw), (_, _, dt) in zip(col_slices, outs):
        if heads is not None and dt == F32 and cw == heads[0] * heads[1] and not (full and n != cw):
            out_specs.append(pl.BlockSpec((tm,) + heads, lambda i, j: (i, 0, 0)))
            out_shape.append(jax.ShapeDtypeStruct((m,) + heads, dt))
        else:
            out_specs.append(pl.BlockSpec((tm, cw), (lambda i, j: (i, j)) if full else (lambda i, j: (i, 0))))
            out_shape.append(jax.ShapeDtypeStruct((m, n if full else cw), dt))
    return pl.pallas_call(
        functools.partial(_proj_kernel, col_slices=col_slices, tile_scale=tile_scale),
        grid=(m // tm, n // tn),
        in_specs=[pl.BlockSpec((tm, k), lambda i, j: (i, 0)),
                  pl.BlockSpec((k, tn), lambda i, j: (0, j))],
        out_specs=out_specs,
        out_shape=out_shape,
        compiler_params=_cparams("parallel", "arbitrary"),
        name=name,
    )(x2d, w)


def _gates_kernel(sm_ref, bg_ref, grow_ref, gcol_ref, *, t_valid):
    z = sm_ref[...] + bg_ref[...]
    col = lax.broadcasted_iota(I32, z.shape, 1)
    tok = lax.broadcasted_iota(I32, z.shape, 0)
    is_ig = col < SMALL_FG
    log_sig = jnp.minimum(z, 0.0) - jnp.log1p(jnp.exp(-jnp.abs(z)))
    gl = jnp.where(is_ig, z, log_sig)
    gl = jnp.where(tok < t_valid, gl, jnp.where(is_ig, NEG_INF, 0.0))
    gl_t = gl.T
    ig_t = gl_t[SMALL_IG:SMALL_IG + 8, :]
    fg_t = gl_t[SMALL_FG:SMALL_FG + 8, :]
    in_chunk = lax.broadcasted_iota(I32, fg_t.shape, 1) % TILE
    shift = 1
    while shift < TILE:
        fg_t = fg_t + jnp.where(in_chunk >= shift, pltpu.roll(fg_t, shift, 1), 0.0)
        shift *= 2
    rows = jnp.concatenate([ig_t - fg_t, fg_t], axis=0)
    grow_ref[...] = rows
    pad = jnp.zeros((LANES - rows.shape[0], rows.shape[1]), F32)
    gcol_ref[...] = jnp.concatenate([rows, pad], axis=0).T


def _gates(small, bg, t_valid):
    b, t, _ = small.shape
    return pl.pallas_call(
        functools.partial(_gates_kernel, t_valid=t_valid),
        grid=(b,),
        in_specs=[pl.BlockSpec((None, t, LANES), lambda bi: (bi, 0, 0)),
                  pl.BlockSpec((1, LANES), lambda bi: (0, 0))],
        out_specs=[pl.BlockSpec((None, 16, t), lambda bi: (bi, 0, 0)),
                   pl.BlockSpec((None, t, LANES), lambda bi: (bi, 0, 0))],
        out_shape=[jax.ShapeDtypeStruct((b, 16, t), F32), jax.ShapeDtypeStruct((b, t, LANES), F32)],
        compiler_params=_cparams("parallel"),
        name="gates",
    )(small, bg)


def _mlstm_kernel(q_ref, k_ref, v_ref, grow_ref, gcol_ref, c0_ref, n0_ref, m0_ref,
                  hn_ref, c_ref, n_ref, m_ref, ct_ref):
    L = q_ref.shape[0]
    c = pl.program_id(1)

    @pl.when(c == 0)
    def _():
        for h in range(A_HEADS):
            ct_ref[h] = c0_ref[h].T
        n_ref[...] = n0_ref[...]
        m_ref[...] = m0_ref[...]

    causal_t = lax.broadcasted_iota(I32, (L, L), 0) <= lax.broadcasted_iota(I32, (L, L), 1)

    for h in range(A_HEADS):
        sl = slice(h * A_HEAD_DIM, (h + 1) * A_HEAD_DIM)
        b_row = grow_ref[SMALL_FG + h:SMALL_FG + h + 1, :]
        c_col = gcol_ref[:, SMALL_IG + h:SMALL_IG + h + 1]
        m_old = m_ref[h][0:1, 0:1]
        q = q_ref[:, sl]
        k = k_ref[:, sl]
        v = v_ref[:, sl]
        ct_old = ct_ref[h]
        n_old = n_ref[h]

        dmat_t = jnp.where(causal_t, b_row + c_col, NEG_INF)
        a_inter = b_row + m_old
        m_j = jnp.maximum(a_inter, jnp.max(dmat_t, axis=0, keepdims=True))
        inter = jnp.exp(a_inter - m_j)
        s_t = (lax.dot_general(k, q, NT_DIMS, preferred_element_type=F32) * A_K_SCALE
               * jnp.exp(dmat_t - m_j))
        num_t = (lax.dot_general(v, s_t.astype(BF16), TN_DIMS, preferred_element_type=F32)
                 + lax.dot_general(ct_old.astype(BF16), q, NT_DIMS, preferred_element_type=F32) * inter)
        n_rows = jnp.broadcast_to(n_old[0:1, :], (16, A_HEAD_DIM)).astype(BF16)
        qn = lax.dot_general(n_rows, q, NT_DIMS, preferred_element_type=F32)[0:1, :]
        den = jnp.sum(s_t, axis=0, keepdims=True) + inter * qn
        hh_t = num_t / jnp.maximum(jnp.abs(den), jnp.exp(-m_j))
        mu = jnp.mean(hh_t, axis=0, keepdims=True)
        var = jnp.mean(jnp.square(hh_t - mu), axis=0, keepdims=True)
        hn_ref[:, sl] = ((hh_t - mu) * lax.rsqrt(var + HEAD_NORM_EPS)).T

        b_last = b_row[:, L - 1:L]
        g_col = b_last + c_col
        m_new = jnp.maximum(b_last + m_old, jnp.max(g_col, axis=0, keepdims=True))
        decay = jnp.exp(b_last + m_old - m_new)
        w_col = jnp.exp(g_col - m_new) * A_K_SCALE
        kw = k.astype(F32) * w_col
        ct_ref[h] = decay * ct_old + lax.dot_general(v, kw.astype(BF16), TN_DIMS,
                                                     preferred_element_type=F32)
        n_ref[h] = decay * n_old + jnp.sum(kw, axis=0, keepdims=True)
        m_ref[h] = jnp.broadcast_to(m_new, (8, LANES))

    @pl.when(c == pl.num_programs(1) - 1)
    def _():
        for h in range(A_HEADS):
            c_ref[h] = ct_ref[h].T


def _mlstm(qkv, small, bg, c0, n0, m0, t_valid):
    b, t, _ = qkv.shape
    assert t % TILE == 0
    grow, gcol = _gates(small, bg, t_valid)
    n0x = jnp.broadcast_to(n0[:, :, None, :], (b, A_HEADS, 8, A_HEAD_DIM))
    m0x = jnp.broadcast_to(m0[:, :, None, None], (b, A_HEADS, 8, LANES))
    qkv_spec = lambda j: pl.BlockSpec((None, TILE, A_WIDTH), lambda bi, ci: (bi, ci, j))
    st_c = pl.BlockSpec((None, A_HEADS, A_HEAD_DIM, A_HEAD_DIM), lambda bi, ci: (bi, 0, 0, 0))
    st_n = pl.BlockSpec((None, A_HEADS, 8, A_HEAD_DIM), lambda bi, ci: (bi, 0, 0, 0))
    st_m = pl.BlockSpec((None, A_HEADS, 8, LANES), lambda bi, ci: (bi, 0, 0, 0))
    hn, c_new, n_new, m_new = pl.pallas_call(
        _mlstm_kernel,
        grid=(b, t // TILE),
        in_specs=[qkv_spec(0), qkv_spec(1), qkv_spec(2),
                  pl.BlockSpec((None, 16, TILE), lambda bi, ci: (bi, 0, ci)),
                  pl.BlockSpec((None, TILE, LANES), lambda bi, ci: (bi, ci, 0)),
                  st_c, st_n, st_m],
        out_specs=[pl.BlockSpec((None, TILE, A_WIDTH), lambda bi, ci: (bi, ci, 0)),
                   st_c, st_n, st_m],
        out_shape=[jax.ShapeDtypeStruct((b, t, A_WIDTH), F32),
                   jax.ShapeDtypeStruct((b, A_HEADS, A_HEAD_DIM, A_HEAD_DIM), F32),
                   jax.ShapeDtypeStruct((b, A_HEADS, 8, A_HEAD_DIM), F32),
                   jax.ShapeDtypeStruct((b, A_HEADS, 8, LANES), F32)],
        scratch_shapes=[pltpu.VMEM((A_HEADS, A_HEAD_DIM, A_HEAD_DIM), F32)],
        compiler_params=_cparams("parallel", "arbitrary"),
        name="mlstm",
    )(qkv, qkv, qkv, grow, gcol, c0, n0x, m0x)
    return hn, c_new, n_new[:, :, 0, :], m_new[:, :, 0, 0]


def _t5_bucket(rel):
    half = N_BUCKETS // 2
    max_exact = half // 2
    ret = jnp.where(rel > 0, half, 0)
    n = jnp.abs(rel)
    nf = jnp.maximum(n, 1).astype(F32)
    large = max_exact + (jnp.log(nf / max_exact) / math.log(MAX_DISTANCE / max_exact)
                         * (half - max_exact)).astype(I32)
    large = jnp.minimum(large, half - 1)
    return ret + jnp.where(n < max_exact, n, large)


def _bias_kernel(rb_ref, out_ref):
    kk = lax.broadcasted_iota(I32, (TILE, TILE), 0)
    qq = lax.broadcasted_iota(I32, (TILE, TILE), 1)
    for ti, off in enumerate((0, -TILE, -2 * TILE)):
        bucket = _t5_bucket(kk + off - qq)
        for h in range(B_HEADS):
            acc = jnp.zeros((TILE, TILE), F32)
            for bkt in range(N_BUCKETS):
                acc = jnp.where(bucket == bkt, rb_ref[bkt, h], acc)
            out_ref[h, ti] = acc * LOG2E


def _bias_tiles(rel_bias):
    return pl.pallas_call(
        _bias_kernel,
        in_specs=[pl.BlockSpec(memory_space=pltpu.SMEM)],
        out_specs=pl.BlockSpec(memory_space=pltpu.VMEM),
        out_shape=jax.ShapeDtypeStruct((B_HEADS, 3, TILE, TILE), F32),
        compiler_params=pltpu.CompilerParams(vmem_limit_bytes=VMEM_LIMIT_BYTES),
        name="bias_tiles",
    )(rel_bias.astype(F32))


def _dsa_kernel(q_ref, qi_ref, sm_ref, k_ref, v_ref, ki_ref, bias_ref, o_ref,
                keys_ref, top_ref, madd_ref, lg_ref, acc_ref, bm_ref, m_ref, l_ref, thr_ref, jsel_ref,
                *, pos0, s_valid, k_sel):
    tq = q_ref.shape[0]
    tk = TILE
    t0 = pos0 + pl.program_id(1) * tq
    nkb = t0 // tk + 1
    qpos = t0 + lax.broadcasted_iota(I32, (1, tq), 1)
    limit = jnp.minimum((qpos // CHUNK + 1) * CHUNK, s_valid)
    blk_iota = lax.broadcasted_iota(I32, (tk, tq), 0)
    idx_bits = int(k_ref.shape[0]).bit_length()

    def rows(kb):
        return pl.ds(pl.multiple_of(kb * tk, tk), tk)

    w_t = sm_ref[...].T
    w_heads = [w_t[SMALL_WI + h:SMALL_WI + h + 1, :] * (IDX_HEADS ** -0.5) * (IDX_DIM ** -0.5)
               for h in range(IDX_HEADS)]

    def score_body(kb, carry):
        ki = ki_ref[rows(kb), :]
        sc = jnp.zeros((tk, tq), F32)
        for h in range(IDX_HEADS):
            qh = qi_ref[:, h * IDX_DIM:(h + 1) * IDX_DIM]
            d = lax.dot_general(ki, qh, NT_DIMS, preferred_element_type=F32)
            sc = sc + w_heads[h] * jnp.maximum(d, 0.0)
        sc = jnp.where(jnp.abs(sc) < F32_MIN_NORMAL, 0.0, sc)
        bits = lax.bitcast_convert_type(sc, I32)
        key = bits ^ ((bits >> 31) & 0x7FFFFFFF)
        adm = (kb * tk + blk_iota) < limit
        keys_ref[rows(kb), :] = jnp.where(adm, key, KEY_NEG_INF)
        top = lax.bitcast_convert_type(bits & jnp.int32(-65536), F32)
        top_ref[rows(kb), :] = jnp.where(adm, top, NEG_INF).astype(BF16)
        return carry

    lax.fori_loop(0, nkb, score_body, 0)

    thr_ref[...] = jnp.full(thr_ref.shape, INT_MIN, I32)
    jsel_ref[...] = jnp.full(jsel_ref.shape, 2 ** 30, I32)

    def count_top(cand16):
        bits16 = cand16 ^ ((cand16 >> 15) & 0x7FFF)
        cand = lax.bitcast_convert_type(jnp.left_shift(bits16, 16), F32)
        cand = jnp.where((cand16 > 0) & (cand16 < KEY_MIN_NORMAL >> 16), F32_MIN_NORMAL, cand)
        cand = jnp.where(cand16 < KEY_NEG_INF >> 16, NEG_INF, cand)
        cand = jnp.where(cand16 > KEY_POS_INF >> 16, -NEG_INF, cand).astype(BF16)
        one, zero = jnp.ones((), BF16), jnp.zeros((), BF16)

        def body(kb, acc):
            ones = jnp.where(top_ref[rows(kb), :] >= cand, one, zero)
            parts = [ones[r * 16:(r + 1) * 16, :] for r in range(tk // 16)]
            while len(parts) > 1:
                parts = [parts[i] + parts[i + 1] for i in range(0, len(parts), 2)]
            return acc + parts[0].astype(F32)
        acc = lax.fori_loop(0, nkb, body, jnp.zeros((16, tq), F32))
        return jnp.sum(acc, axis=0, keepdims=True)

    def count_many(preds, n_preds):
        def body(kb, accs):
            hits = preds(keys_ref[rows(kb), :], kb * tk + blk_iota)
            out = []
            for acc, hit in zip(accs, hits):
                ones = jnp.where(hit, 1.0, 0.0)
                parts = [ones[r * 8:(r + 1) * 8, :] for r in range(tk // 8)]
                while len(parts) > 1:
                    parts = [parts[i] + parts[i + 1] for i in range(0, len(parts), 2)]
                out.append(acc + parts[0])
            return tuple(out)
        accs = lax.fori_loop(0, nkb, body, tuple(jnp.zeros((8, tq), F32) for _ in range(n_preds)))
        return [jnp.sum(acc, axis=0, keepdims=True) for acc in accs]

    def count(pred):
        return count_many(lambda blk, idx: (pred(blk, idx),), 1)[0]

    @pl.when(nkb * tk > k_sel)
    def _():
        kf = float(k_sel)
        zero = jnp.zeros((1, tq), I32)
        cnt = count_top(zero)
        top = jnp.where(cnt >= kf, zero, jnp.full((1, tq), -2 ** 15, I32))

        def top_body(i, top):
            cand = top + jnp.left_shift(jnp.int32(1), 14 - i)
            return jnp.where(count_top(cand) >= kf, cand, top)

        top = lax.fori_loop(0, 15, top_body, top)

        def bit_body(i, thr):
            cand = thr + jnp.left_shift(jnp.int32(1), 15 - i)
            cnt = count(lambda blk, _: blk >= cand)
            return jnp.where(cnt >= kf, cand, thr)

        thr = lax.fori_loop(0, 16, bit_body, jnp.left_shift(top, 16))
        thr_ref[...] = jnp.broadcast_to(thr, thr_ref.shape)
        n_gt, n_ge = count_many(lambda blk, _: (blk > thr, blk >= thr), 2)
        need = kf - n_gt
        split = jnp.max(jnp.where(n_ge - n_gt > need, 1.0, 0.0)) > 0.0

        @pl.when(split)
        def _():
            def jbit_body(i, j0):
                cand = j0 + jnp.left_shift(jnp.int32(1), idx_bits - 1 - i)
                f = count(lambda blk, idx: (blk == thr) & (idx < cand))
                return jnp.where(f < need, cand, j0)

            j0 = lax.fori_loop(0, idx_bits, jbit_body, jnp.zeros((1, tq), I32))
            jsel_ref[...] = jnp.broadcast_to(j0 + 1, jsel_ref.shape)

    thr = thr_ref[0:1, :]
    jsel = jsel_ref[0:1, :]

    def mask_body(kb, carry):
        blk = keys_ref[rows(kb), :]
        idx = kb * tk + blk_iota
        sel = ((blk > thr) | ((blk == thr) & (idx < jsel))) & (idx < limit)
        madd_ref[rows(kb), :] = jnp.where(sel, 0.0, NEG_INF)
        return carry

    lax.fori_loop(0, nkb, mask_body, 0)

    m_ref[...] = jnp.full(m_ref.shape, NEG_INF, F32)
    l_ref[...] = jnp.zeros(l_ref.shape, F32)
    acc_ref[...] = jnp.zeros(acc_ref.shape, F32)

    def attn_body(kb, carry, near):
        for h in range(B_HEADS):
            sl = slice(h * B_HEAD_DIM, (h + 1) * B_HEAD_DIM)
            lt = lax.dot_general(k_ref[rows(kb), sl], q_ref[:, sl], NT_DIMS, preferred_element_type=F32)
            lt = lt + madd_ref[rows(kb), :]
            if near:
                lt = lt + bias_ref[h, nkb - 1 - kb, :, :tq]
            lg_ref[h] = lt
            bm_ref[h:h + 1, :] = jnp.max(lt, axis=0, keepdims=True)
        for h in range(B_HEADS):
            sl = slice(h * B_HEAD_DIM, (h + 1) * B_HEAD_DIM)
            m_old = m_ref[h:h + 1, :]
            m_new = jnp.maximum(m_old, bm_ref[h:h + 1, :])
            m_safe = jnp.where(m_new == NEG_INF, 0.0, m_new)
            alpha = jnp.exp2(m_old - m_safe)
            p = jnp.exp2(lg_ref[h] - m_safe)
            l_ref[h:h + 1, :] = alpha * l_ref[h:h + 1, :] + jnp.sum(p, axis=0, keepdims=True)
            pv = lax.dot_general(v_ref[rows(kb), sl], p.astype(BF16), TN_DIMS,
                                 preferred_element_type=F32)
            acc_ref[h] = alpha * acc_ref[h] + pv
            m_ref[h:h + 1, :] = m_new
        return carry

    n_far = jnp.maximum(nkb - 2, 0)
    lax.fori_loop(0, n_far, functools.partial(attn_body, near=False), 0)
    for h in range(B_HEADS):
        m_ref[h:h + 1, :] = m_ref[h:h + 1, :] + bias_ref[h, 2, 0:1, :tq]
    lax.fori_loop(n_far, nkb, functools.partial(attn_body, near=True), 0)
    for h in range(B_HEADS):
        sl = slice(h * B_HEAD_DIM, (h + 1) * B_HEAD_DIM)
        o_ref[:, sl] = (acc_ref[h] / l_ref[h:h + 1, :]).T


def _dsa(q, qi, small, k, v, ki, bias, pos0, s_valid, k_sel, n_query):
    b, tq_all, n_wide = q.shape
    s_pad = k.shape[1]
    tq = LANES if n_query <= LANES else TILE
    n_tiles = -(-n_query // tq)
    assert s_pad % TILE == 0 and pos0 % TILE == 0 and (tq == TILE or n_tiles == 1)
    assert n_tiles * tq <= tq_all and pos0 + n_tiles * tq <= s_pad and n_wide % B_WIDTH == 0
    q_block = n_wide // B_WIDTH - 1
    qspec = lambda w: pl.BlockSpec((None, tq, w), lambda bi, ji: (bi, ji, 0))
    kspec = lambda w: pl.BlockSpec((None, s_pad, w), lambda bi, ji: (bi, 0, 0))
    return pl.pallas_call(
        functools.partial(_dsa_kernel, pos0=pos0, s_valid=s_valid, k_sel=k_sel),
        grid=(b, n_tiles),
        in_specs=[pl.BlockSpec((None, tq, B_WIDTH), lambda bi, ji: (bi, ji, q_block)),
                  qspec(IDX_HEADS * IDX_DIM), qspec(LANES),
                  kspec(B_WIDTH), kspec(B_WIDTH), kspec(IDX_DIM),
                  pl.BlockSpec((B_HEADS, 3, TILE, TILE), lambda bi, ji: (0, 0, 0, 0))],
        out_specs=qspec(B_WIDTH),
        out_shape=jax.ShapeDtypeStruct((b, n_tiles * tq, B_WIDTH), F32),
        scratch_shapes=[pltpu.VMEM((s_pad, tq), I32),
                        pltpu.VMEM((s_pad, tq), BF16),
                        pltpu.VMEM((s_pad, tq), F32),
                        pltpu.VMEM((B_HEADS, TILE, tq), F32),
                        pltpu.VMEM((B_HEADS, B_HEAD_DIM, tq), F32),
                        pltpu.VMEM((B_HEADS, tq), F32),
                        pltpu.VMEM((B_HEADS, tq), F32),
                        pltpu.VMEM((B_HEADS, tq), F32),
                        pltpu.VMEM((8, tq), I32),
                        pltpu.VMEM((8, tq), I32)],
        compiler_params=_cparams("parallel", "arbitrary"),
        name="dsa",
    )(q, qi, small, k, v, ki, bias)


def _pack_kernel(cache_ref, new_ref, o_ref, *, n_cache_blocks):
    j = pl.program_id(1)

    @pl.when(j < n_cache_blocks)
    def _():
        o_ref[...] = cache_ref[...].reshape(o_ref.shape).astype(o_ref.dtype)

    @pl.when(j >= n_cache_blocks)
    def _():
        o_ref[...] = new_ref[...]


def _pack_rows(cache, new16):
    b, p, hh, dd = cache.shape
    rb = math.gcd(p, PACK_ROWS)
    new16 = _pad_rows(new16, -(-new16.shape[1] // rb) * rb)
    tn = new16.shape[1]
    assert rb % TILE == 0 and new16.shape[2] == hh * dd
    ncb = p // rb
    return pl.pallas_call(
        functools.partial(_pack_kernel, n_cache_blocks=ncb),
        grid=(b, (p + tn) // rb),
        in_specs=[pl.BlockSpec((None, rb, hh, dd), lambda bi, ji: (bi, jnp.minimum(ji, ncb - 1), 0, 0)),
                  pl.BlockSpec((None, rb, hh * dd), lambda bi, ji: (bi, jnp.maximum(ji - ncb, 0), 0))],
        out_specs=pl.BlockSpec((None, rb, hh * dd), lambda bi, ji: (bi, ji, 0)),
        out_shape=jax.ShapeDtypeStruct((b, p + tn, hh * dd), BF16),
        compiler_params=_cparams("parallel", "arbitrary"),
        name="pack_rows",
    )(cache, new16)


def _merge_kernel(x_ref, hn_ref, ob_ref, wg_ref, woa_ref, wob_ref, wo_ref, vec_ref, y_ref):
    x = x_ref[...]
    xb = x.astype(BF16)
    gate = lambda i: jnp.dot(xb, wg_ref[i], preferred_element_type=F32)
    hn = hn_ref[...] * vec_ref[0:1, :]
    branch_a = (hn * jax.nn.sigmoid(gate(0))) * jax.nn.silu(gate(1))
    branch_b = ob_ref[...] * jax.nn.silu(gate(2))
    mixed = (jax.nn.sigmoid(gate(3)) * jnp.dot(branch_a.astype(BF16), woa_ref[...], preferred_element_type=F32)
             + jax.nn.sigmoid(gate(4)) * jnp.dot(branch_b.astype(BF16), wob_ref[...], preferred_element_type=F32))
    y = DN_ALPHA * x + jnp.dot(mixed.astype(BF16), wo_ref[...], preferred_element_type=F32)
    mu = jnp.mean(y, axis=1, keepdims=True)
    var = jnp.mean(jnp.square(y - mu), axis=1, keepdims=True)
    y_ref[...] = (y - mu) * lax.rsqrt(var + LN_EPS) * vec_ref[1:2, :] + vec_ref[2:3, :]


def _merge(x2d, hn2d, ob2d, wg, woa, wob, wo, vec):
    m, d = x2d.shape
    tm = min(MERGE_TM, m)
    assert m % tm == 0
    row = pl.BlockSpec((tm, d), lambda i: (i, 0))
    const = lambda shape: pl.BlockSpec(shape, lambda i: (0,) * len(shape))
    return pl.pallas_call(
        _merge_kernel,
        grid=(m // tm,),
        in_specs=[row, row, row, const(wg.shape), const(woa.shape), const(wob.shape),
                  const(wo.shape), const(vec.shape)],
        out_specs=row,
        out_shape=jax.ShapeDtypeStruct((m, d), F32),
        compiler_params=_cparams("parallel"),
        name="merge",
    )(x2d, hn2d, ob2d, wg, woa, wob, wo, vec)


def _split_weights(w_in, b_gates):
    cuts = [0]
    for c in COL_SIZES:
        cuts.append(cuts[-1] + c)
    cols = [w_in[:, cuts[i]:cuts[i + 1]] for i in range(len(COL_SIZES))]
    (qa, ka, va, oa, za, ia, fa, qb, kb, vb, zb, qi, ki, wi, ga, gb) = cols
    pad = lambda w, n: jnp.pad(w, ((0, 0), (0, n - w.shape[1])))
    small = jnp.concatenate([pad(ia, SMALL_FG - SMALL_IG), pad(fa, SMALL_WI - SMALL_FG),
                             pad(wi, LANES - SMALL_WI)], axis=1)
    bg = jnp.concatenate([jnp.pad(b_gates[:A_HEADS], (0, SMALL_FG - SMALL_IG - A_HEADS)),
                          jnp.pad(b_gates[A_HEADS:], (0, LANES - SMALL_FG - A_HEADS))])[None, :]
    c16 = lambda w: w.astype(BF16)
    return dict(wide=c16(jnp.concatenate([qa, ka, va, qb], axis=1)), kb=c16(kb), vb=c16(vb),
                narrow=c16(jnp.concatenate([qi, small, ki], axis=1)), bg=bg.astype(F32),
                gates=jnp.stack([c16(oa), c16(za), c16(zb), c16(ga), c16(gb)]))


def _pad_rows(a, n):
    return jnp.pad(a, ((0, 0), (0, n - a.shape[1]), (0, 0)))


def _group(x, w, consts, state, cache):
    b, t, d = x.shape
    x2d = x.reshape(b * t, d)
    n_wide = w["wide"].shape[1]
    n_qi = IDX_HEADS * IDX_DIM
    assert B_WIDTH == PROJ_TN
    (wide,) = _project(x2d, w["wide"], [(0, n_wide, BF16)], "proj_wide",
                       tile_scale=(n_wide // PROJ_TN - 1, B_Q_SCALE))
    bh = (B_HEADS, B_HEAD_DIM)
    kb32, kb16 = _project(x2d, w["kb"], [(0, B_WIDTH, F32), (0, B_WIDTH, BF16)], "proj_kb", heads=bh)
    vb32, vb16 = _project(x2d, w["vb"], [(0, B_WIDTH, F32), (0, B_WIDTH, BF16)], "proj_vb", heads=bh)
    qi, small, ki32, ki16 = _project(
        x2d, w["narrow"],
        [(0, n_qi, BF16), (n_qi, LANES, F32), (n_qi + LANES, IDX_DIM, F32), (n_qi + LANES, IDX_DIM, BF16)],
        "proj_narrow")
    r3 = lambda a: a.reshape(b, t, a.shape[-1])
    wide, kb16, vb16, qi, ki16, small = map(r3, (wide, kb16, vb16, qi, ki16, small))

    t_pad = -(-t // TILE) * TILE
    wide = _pad_rows(wide, t_pad)
    c0, n0, m0 = state
    hn, c_new, n_new, m_new = _mlstm(wide, _pad_rows(small, t_pad), w["bg"], c0, n0, m0, t_valid=t)

    if cache is None:
        pos0, keys_k, keys_v, keys_i = 0, kb16, vb16, ki16
    else:
        cache_k, cache_v, cache_i = cache
        pos0 = cache_k.shape[1]
        keys_k = _pack_rows(cache_k, kb16)
        keys_v = _pack_rows(cache_v, vb16)
        keys_i = jnp.concatenate([cache_i.astype(BF16), ki16], axis=1)
    s_valid = pos0 + t
    s_pad = max(pos0 + t_pad, keys_k.shape[1])
    k_sel = min(TOPK_MAX, s_valid // 4)
    ob = _dsa(wide, _pad_rows(qi, t_pad), _pad_rows(small, t_pad),
              _pad_rows(keys_k, s_pad), _pad_rows(keys_v, s_pad), _pad_rows(keys_i, s_pad),
              consts["bias"], pos0=pos0, s_valid=s_valid, k_sel=k_sel, n_query=t)

    y = _merge(x2d, hn[:, :t].reshape(b * t, A_WIDTH), ob[:, :t].reshape(b * t, B_WIDTH),
               w["gates"], consts["woa"], consts["wob"], consts["wo"], consts["vec"])
    return (y.reshape(b, t, d), kb32.reshape(b, t, B_HEADS, B_HEAD_DIM),
            vb32.reshape(b, t, B_HEADS, B_HEAD_DIM), ki32.reshape(b, t, IDX_DIM),
            c_new, n_new, m_new)


def kernel(x_prompt, x_sample, cache_k, cache_v, cache_idx_k, state_C, state_n, state_m,
           w_in, b_gates, a_norm_g, w_out_a, w_out_b, w_o, rel_bias, ln_g, ln_b):
    w = _split_weights(w_in, b_gates)
    consts = dict(bias=_bias_tiles(rel_bias), woa=w_out_a.astype(BF16), wob=w_out_b.astype(BF16),
                  wo=w_o.astype(BF16),
                  vec=jnp.pad(jnp.stack([a_norm_g, ln_g, ln_b]).astype(F32), ((0, 5), (0, 0))))
    bp = x_prompt.shape[0]
    zero_state = (jnp.zeros((bp, A_HEADS, A_HEAD_DIM, A_HEAD_DIM), F32),
                  jnp.zeros((bp, A_HEADS, A_HEAD_DIM), F32), jnp.zeros((bp, A_HEADS), F32))
    y_p, k_p, v_p, i_p, c_p, n_p, m_p = _group(x_prompt, w, consts, zero_state, None)
    y_s, k_s, v_s, i_s, c_s, n_s, m_s = _group(
        x_sample, w, consts,
        (state_C.astype(F32), state_n.astype(F32), state_m.astype(F32)),
        (cache_k, cache_v, cache_idx_k))
    return (y_p, y_s, k_p, v_p, i_p, c_p, n_p, m_p, k_s, v_s, i_s, c_s, n_s, m_s)
```
